```python
import math
import jax, jax.numpy as jnp
from jax import lax
import numpy as np

D_MODEL = 1024
BATCH = 8
SEQ = 2048
DEPTH = 2

N_MEM = 256
HEAD_DIM = 128
DN_HEADS = D_MODEL // HEAD_DIM
FOX_HEADS = D_MODEL // HEAD_DIM
MEM_HEADS = 4
MIX_WIDTH = D_MODEL
MEM_WIDTH = MEM_HEADS * HEAD_DIM
D_FF = 4 * D_MODEL
CONV_WIDTH = 4
CHUNK = 64
Q_BLOCK = 128
EPS = 1e-6
N_DN = (DEPTH + 1) // 2
N_FOX = DEPTH // 2
DN_IN = 4 * MIX_WIDTH + 2 * DN_HEADS + MEM_WIDTH
FOX_IN = 4 * MIX_WIDTH + FOX_HEADS + MEM_WIDTH
OUT_IN = MIX_WIDTH + MEM_WIDTH

kernel_name = "hybrid_deltanet_fox_memory_decoder"


def rms_norm(x, w):
    xf = x.astype(jnp.float32)
    y = xf * lax.rsqrt(jnp.mean(xf * xf, axis=-1, keepdims=True) + EPS)
    return (y * w.astype(jnp.float32)).astype(x.dtype)


def l2_norm(x):
    xf = x.astype(jnp.float32)
    return xf * lax.rsqrt(jnp.sum(xf * xf, axis=-1, keepdims=True) + EPS)


def split_heads(t, heads):
    return t.reshape(t.shape[:-1] + (heads, HEAD_DIM))


def causal_depthwise_conv(x, w):
    c = x.shape[-1]
    return lax.conv_general_dilated(
        x, w[:, None, :].astype(x.dtype), window_strides=(1,),
        padding=[(CONV_WIDTH - 1, 0)], dimension_numbers=("NWC", "WIO", "NWC"),
        feature_group_count=c)


def chunked_gated_delta_rule(q, k, v, g, beta):
    bsz, seq, heads, _ = q.shape
    n = seq // CHUNK

    def chunks(t):
        return t.reshape(bsz, n, CHUNK, heads, -1).transpose(1, 0, 3, 2, 4)

    qc, kc, vc = chunks(q), chunks(k), chunks(v)
    gc = chunks(g[..., None])[..., 0]
    bc = chunks(beta[..., None])[..., 0]
    gcum = jnp.cumsum(gc, axis=-1)
    causal = jnp.tril(jnp.ones((CHUNK, CHUNK), dtype=bool))
    strict = jnp.tril(jnp.ones((CHUNK, CHUNK), dtype=bool), -1)
    diff = gcum[..., :, None] - gcum[..., None, :]
    decay = jnp.where(causal, jnp.exp(jnp.where(causal, diff, 0.0)), 0.0)
    kb = kc * bc[..., None]
    a_mat = jnp.where(strict, jnp.einsum("nbhik,nbhjk->nbhij", kb, kc) * decay, 0.0)
    eye = jnp.eye(CHUNK, dtype=jnp.float32)
    t_mat = lax.linalg.triangular_solve(
        eye + a_mat, jnp.broadcast_to(eye, a_mat.shape), left_side=True, lower=True)
    u = jnp.einsum("nbhij,nbhjv->nbhiv", t_mat, vc * bc[..., None])
    w = jnp.einsum("nbhij,nbhjk->nbhik", t_mat, kb * jnp.exp(gcum)[..., None])
    qk = jnp.where(causal, jnp.einsum("nbhik,nbhjk->nbhij", qc, kc) * decay, 0.0)

    def step(state, inp):
        q_i, k_i, u_i, w_i, qk_i, g_i = inp
        v_new = u_i - jnp.einsum("bhck,bhkv->bhcv", w_i, state)
        out = (jnp.einsum("bhck,bhkv->bhcv", q_i * jnp.exp(g_i)[..., None], state)
               + jnp.einsum("bhij,bhjv->bhiv", qk_i, v_new))
        g_last = g_i[..., -1:]
        k_dec = k_i * jnp.exp(g_last - g_i)[..., None]
        state = state * jnp.exp(g_last)[..., None] + jnp.einsum("bhck,bhcv->bhkv", k_dec, v_new)
        return state, out

    state0 = jnp.zeros((bsz, heads, q.shape[-1], v.shape[-1]), jnp.float32)
    _, o = lax.scan(step, state0, (qc, kc, u, w, qk, gcum))
    return o.transpose(1, 0, 3, 2, 4).reshape(bsz, seq, heads, -1)


def gated_deltanet(h, w_in, conv_w, a_log, dt_bias, o_norm_w):
    bsz, seq, _ = h.shape
    proj = h @ w_in
    qkv = proj[..., : 3 * MIX_WIDTH]
    z = proj[..., 3 * MIX_WIDTH: 4 * MIX_WIDTH]
    a = proj[..., 4 * MIX_WIDTH: 4 * MIX_WIDTH + DN_HEADS]
    b = proj[..., 4 * MIX_WIDTH + DN_HEADS: 4 * MIX_WIDTH + 2 * DN_HEADS]
    q_mem = proj[..., 4 * MIX_WIDTH + 2 * DN_HEADS:]
    qkv = jax.nn.silu(causal_depthwise_conv(qkv, conv_w))
    q = l2_norm(split_heads(qkv[..., :MIX_WIDTH], DN_HEADS)) * (HEAD_DIM ** -0.5)
    k = l2_norm(split_heads(qkv[..., MIX_WIDTH: 2 * MIX_WIDTH], DN_HEADS))
    v = split_heads(qkv[..., 2 * MIX_WIDTH:], DN_HEADS).astype(jnp.float32)
    beta = jax.nn.sigmoid(b.astype(jnp.float32))
    g = -jnp.exp(a_log.astype(jnp.float32)) * jax.nn.softplus(
        a.astype(jnp.float32) + dt_bias.astype(jnp.float32))
    o = chunked_gated_delta_rule(q, k, v, g, beta)
    o = rms_norm(o, o_norm_w) * jax.nn.silu(split_heads(z, DN_HEADS).astype(jnp.float32))
    return o.reshape(bsz, seq, MIX_WIDTH).astype(h.dtype), q_mem


def forgetting_attention(h, w_in, f_bias, q_norm_w, k_norm_w):
    bsz, seq, _ = h.shape
    proj = h @ w_in
    q = split_heads(proj[..., :MIX_WIDTH], FOX_HEADS)
    k = split_heads(proj[..., MIX_WIDTH: 2 * MIX_WIDTH], FOX_HEADS)
    v = split_heads(proj[..., 2 * MIX_WIDTH: 3 * MIX_WIDTH], FOX_HEADS)
    gate = proj[..., 3 * MIX_WIDTH: 4 * MIX_WIDTH]
    f_logit = proj[..., 4 * MIX_WIDTH: 4 * MIX_WIDTH + FOX_HEADS]
    q_mem = proj[..., 4 * MIX_WIDTH + FOX_HEADS:]
    q = rms_norm(q, q_norm_w).astype(jnp.float32) * (HEAD_DIM ** -0.5)
    k = rms_norm(k, k_norm_w).astype(jnp.float32)
    log_f = jax.nn.log_sigmoid(f_logit.astype(jnp.float32) + f_bias.astype(jnp.float32))
    f_cum = jnp.cumsum(log_f, axis=1).transpose(0, 2, 1)
    nb = seq // Q_BLOCK
    q_blocks = q.reshape(bsz, nb, Q_BLOCK, FOX_HEADS, HEAD_DIM).transpose(1, 0, 2, 3, 4)
    f_blocks = f_cum.reshape(bsz, FOX_HEADS, nb, Q_BLOCK).transpose(2, 0, 1, 3)
    k_pos = jnp.arange(seq)

    def block(args):
        qb, fb, i = args
        s = jnp.einsum("bqhd,bkhd->bhqk", qb, k)
        bias = fb[..., :, None] - f_cum[:, :, None, :]
        q_pos = i * Q_BLOCK + jnp.arange(Q_BLOCK)
        mask = k_pos[None, :] <= q_pos[:, None]
        p = jax.nn.softmax(jnp.where(mask, s + bias, -jnp.inf), axis=-1)
        return jnp.einsum("bhqk,bkhd->bqhd", p.astype(v.dtype), v)

    o = lax.map(block, (q_blocks, f_blocks, jnp.arange(nb)))
    o = o.transpose(1, 0, 2, 3, 4).reshape(bsz, seq, MIX_WIDTH)
    o = o.astype(jnp.float32) * jax.nn.sigmoid(gate.astype(jnp.float32))
    return o.astype(h.dtype), q_mem


def memory_attention(q_mem, mem_k, mem_v, q_norm_w):
    bsz, seq, _ = q_mem.shape
    q = rms_norm(split_heads(q_mem, MEM_HEADS), q_norm_w).astype(jnp.float32) * (HEAD_DIM ** -0.5)
    p = jax.nn.softmax(jnp.einsum("bthd,bmhd->bhtm", q, mem_k), axis=-1)
    o = jnp.einsum("bhtm,bmhd->bthd", p.astype(mem_v.dtype), mem_v)
    return o.reshape(bsz, seq, MEM_WIDTH).astype(q_mem.dtype)


def setup_inputs(seed: int = 0) -> dict:
    key = jax.random.key(seed)
    ks = jax.random.split(key, 20)
    f32 = jnp.float32

    def dense(k, shape, fan_in):
        return jax.random.normal(k, shape, f32) * fan_in ** -0.5

    def gain(k, shape):
        return 1.0 + 0.02 * jax.random.normal(k, shape, f32)

    dt = jnp.exp(jax.random.uniform(ks[9], (N_DN, DN_HEADS), f32, math.log(1e-3), math.log(1e-1)))
    return {
        "x": jax.random.normal(ks[0], (BATCH, SEQ, D_MODEL), f32),
        "mem": jax.random.normal(ks[1], (BATCH, N_MEM, D_MODEL), f32),
        "mem_norm_w": gain(ks[2], (D_MODEL,)),
        "w_mem_kv": dense(ks[3], (D_MODEL, 2 * MEM_WIDTH), D_MODEL),
        "mem_k_norm_w": gain(ks[4], (HEAD_DIM,)),
        "norm1_w": gain(ks[5], (DEPTH, D_MODEL)),
        "dn_w_in": dense(ks[6], (N_DN, D_MODEL, DN_IN), D_MODEL),
        "dn_conv_w": dense(ks[7], (N_DN, CONV_WIDTH, 3 * MIX_WIDTH), CONV_WIDTH),
        "dn_a_log": jnp.log(jax.random.uniform(ks[8], (N_DN, DN_HEADS), f32, 1.0, 16.0)),
        "dn_dt_bias": dt + jnp.log(-jnp.expm1(-dt)),
        "dn_o_norm_w": gain(ks[10], (N_DN, HEAD_DIM)),
        "fox_w_in": dense(ks[11], (N_FOX, D_MODEL, FOX_IN), D_MODEL),
        "fox_f_bias": jax.random.uniform(ks[12], (N_FOX, FOX_HEADS), f32, 1.0, 4.0),
        "fox_q_norm_w": gain(ks[13], (N_FOX, HEAD_DIM)),
        "fox_k_norm_w": gain(ks[14], (N_FOX, HEAD_DIM)),
        "memq_norm_w": gain(ks[15], (DEPTH, HEAD_DIM)),
        "w_out": dense(ks[16], (DEPTH, OUT_IN, D_MODEL), OUT_IN),
        "norm2_w": gain(ks[17], (DEPTH, D_MODEL)),
        "w_mlp1": dense(ks[18], (DEPTH, D_MODEL, D_FF), D_MODEL),
        "w_mlp2": dense(ks[19], (DEPTH, D_FF, D_MODEL), D_FF),
    }


def reference(x, mem, mem_norm_w, w_mem_kv, mem_k_norm_w, norm1_w, dn_w_in, dn_conv_w,
              dn_a_log, dn_dt_bias, dn_o_norm_w, fox_w_in, fox_f_bias, fox_q_norm_w,
              fox_k_norm_w, memq_norm_w, w_out, norm2_w, w_mlp1, w_mlp2):
    mem_kv = rms_norm(mem, mem_norm_w) @ w_mem_kv
    mem_k = rms_norm(split_heads(mem_kv[..., :MEM_WIDTH], MEM_HEADS), mem_k_norm_w).astype(jnp.float32)
    mem_v = split_heads(mem_kv[..., MEM_WIDTH:], MEM_HEADS)
    for i in range(DEPTH):
        h = rms_norm(x, norm1_w[i])
        j = i // 2
        if i % 2 == 0:
            mix, q_mem = gated_deltanet(h, dn_w_in[j], dn_conv_w[j], dn_a_log[j],
                                        dn_dt_bias[j], dn_o_norm_w[j])
        else:
            mix, q_mem = forgetting_attention(h, fox_w_in[j], fox_f_bias[j],
                                              fox_q_norm_w[j], fox_k_norm_w[j])
        mem_out = memory_attention(q_mem, mem_k, mem_v, memq_norm_w[i])
        x = x + jnp.concatenate([mix, mem_out], axis=-1) @ w_out[i]
        h = rms_norm(x, norm2_w[i])
        x = x + jnp.square(jax.nn.relu(h @ w_mlp1[i])) @ w_mlp2[i]
    return x
```

```python
import functools

import jax
import jax.numpy as jnp
from jax import lax
from jax.experimental import pallas as pl
from jax.experimental.pallas import tpu as pltpu

F32 = jnp.float32
BF16 = jnp.bfloat16

HEAD_DIM = 128
LANES = 128
MEM_HEADS = 4
CONV_WIDTH = 4
EPS = 1e-6
DN_CHUNK = 64
VMEM_LIMIT = 56 * 1024 * 1024


def _sigmoid(x):
    return 1.0 / (1.0 + jnp.exp(-x))


def _softplus(x):
    return jnp.maximum(x, 0.0) + jnp.log1p(jnp.exp(-jnp.abs(x)))


def _rms(x, w):
    return x * lax.rsqrt(jnp.mean(x * x, axis=-1, keepdims=True) + EPS) * w


def _dot(a, b):
    return jnp.dot(a, b, preferred_element_type=F32)


def _dot_nt(a, b):
    return lax.dot_general(a, b, (((1,), (1,)), ((), ())), preferred_element_type=F32)


def _split3(x):
    hi = x.astype(BF16).astype(F32)
    r = x - hi
    mid = r.astype(BF16).astype(F32)
    lo = (r - mid).astype(BF16).astype(F32)
    return hi, mid, lo


def _diff_operands(g, lane):
    hi, mid, lo = _split3(g)
    one = jnp.where(lane < 6, 1.0, 0.0)
    a = jnp.where(lane == 0, hi, jnp.where(lane == 1, mid, jnp.where(lane == 2, lo, one)))
    b = jnp.where(lane == 3, -hi, jnp.where(lane == 4, -mid, jnp.where(lane == 5, -lo, one)))
    return a.astype(BF16), b.astype(BF16)


def _chunk_cumsum(x, row_in_chunk, length):
    s = 1
    while s < length:
        x = x + jnp.where(row_in_chunk >= s, pltpu.roll(x, s, 0), 0.0)
        s *= 2
    return x


def _head_column(x, lane, idx):
    col = jnp.sum(jnp.where(lane == idx, x, 0.0), axis=1, keepdims=True)
    return jnp.broadcast_to(col, x.shape)


def _const_spec(shape):
    return pl.BlockSpec(shape, lambda *_: (0,) * len(shape), pipeline_mode=pl.Buffered(1))


def _mem_kv_kernel(mem_ref, nw_ref, w_ref, knw_ref, k_ref, v_ref):
    h = _rms(mem_ref[0], nw_ref[...]).astype(BF16)
    kv = _dot(h, w_ref[...])
    mw = kv.shape[1] // 2
    for hd in range(mw // HEAD_DIM):
        sl = slice(hd * HEAD_DIM, (hd + 1) * HEAD_DIM)
        k_ref[0, :, sl] = _rms(kv[:, sl], knw_ref[...]).astype(BF16)
    v_ref[0] = kv[:, mw:].astype(BF16)


def _mem_kv(mem, mem_norm_w, w_mem_kv, mem_k_norm_w):
    b, n_mem, d = mem.shape
    mw = w_mem_kv.shape[1] // 2
    return pl.pallas_call(
        _mem_kv_kernel,
        grid=(b,),
        in_specs=[
            pl.BlockSpec((1, n_mem, d), lambda i: (i, 0, 0)),
            _const_spec((1, d)),
            _const_spec((d, 2 * mw)),
            _const_spec((1, HEAD_DIM)),
        ],
        out_specs=[
            pl.BlockSpec((1, n_mem, mw), lambda i: (i, 0, 0)),
            pl.BlockSpec((1, n_mem, mw), lambda i: (i, 0, 0)),
        ],
        out_shape=[jax.ShapeDtypeStruct((b, n_mem, mw), BF16)] * 2,
        compiler_params=pltpu.CompilerParams(
            dimension_semantics=("arbitrary",), vmem_limit_bytes=VMEM_LIMIT),
        name="mem_kv",
    )(mem, mem_norm_w.reshape(1, d), w_mem_kv.astype(BF16), mem_k_norm_w.reshape(1, HEAD_DIM))


def _mem_attention(qm, mqw, mk_ref, mv_ref, mo_ref):
    scale = HEAD_DIM ** -0.5
    for hd in range(MEM_HEADS):
        sl = slice(hd * HEAD_DIM, (hd + 1) * HEAD_DIM)
        q = (_rms(qm[:, sl], mqw) * scale).astype(BF16)
        s = _dot_nt(q, mk_ref[0, :, sl])
        p = jnp.exp(s - jnp.max(s, axis=1, keepdims=True))
        o = _dot(p.astype(BF16), mv_ref[0, :, sl]) / jnp.sum(p, axis=1, keepdims=True)
        mo_ref[:, sl] = o.astype(BF16)


def _in_proj_dn_kernel(x_ref, nw_ref, wmain_ref, wab_ref, wqm_ref, alog_ref, dtb_ref, mqw_ref,
                       mk_ref, mv_ref, qkv_ref, z_ref, gb_ref, mo_ref, *, n_chunk, heads):
    h = _rms(x_ref[...], nw_ref[...]).astype(BF16)
    qkv_w = qkv_ref.shape[1]
    for n0 in range(0, qkv_w, n_chunk):
        qkv_ref[:, n0:n0 + n_chunk] = _dot(h, wmain_ref[:, n0:n0 + n_chunk]).astype(BF16)
    for n0 in range(0, z_ref.shape[1], n_chunk):
        z_ref[:, n0:n0 + n_chunk] = _dot(
            h, wmain_ref[:, qkv_w + n0:qkv_w + n0 + n_chunk]).astype(BF16)
    ab = _dot(h, wab_ref[...])
    lane = lax.broadcasted_iota(jnp.int32, ab.shape, 1)
    g = -jnp.exp(alog_ref[...]) * _softplus(ab + dtb_ref[...])
    gb_ref[...] = jnp.where(lane < heads, g, _sigmoid(ab))
    _mem_attention(_dot(h, wqm_ref[...]), mqw_ref[...], mk_ref, mv_ref, mo_ref)


def _in_proj_fox_kernel(x_ref, nw_ref, wmain_ref, wf_ref, wqm_ref, fb_ref, qnw_ref, knw_ref,
                        mqw_ref, mk_ref, mv_ref, q_ref, k_ref, v_ref, gate_ref, lf_ref, mo_ref,
                        *, n_chunk):
    h = _rms(x_ref[...], nw_ref[...]).astype(BF16)
    width = q_ref.shape[1]
    scale = HEAD_DIM ** -0.5
    for n0 in range(0, width, n_chunk):
        yq = _dot(h, wmain_ref[:, n0:n0 + n_chunk])
        yk = _dot(h, wmain_ref[:, width + n0:width + n0 + n_chunk])
        for c0 in range(0, n_chunk, HEAD_DIM):
            q_ref[:, n0 + c0:n0 + c0 + HEAD_DIM] = (
                _rms(yq[:, c0:c0 + HEAD_DIM], qnw_ref[...]) * scale).astype(BF16)
            k_ref[:, n0 + c0:n0 + c0 + HEAD_DIM] = _rms(
                yk[:, c0:c0 + HEAD_DIM], knw_ref[...]).astype(BF16)
        v_ref[:, n0:n0 + n_chunk] = _dot(
            h, wmain_ref[:, 2 * width + n0:2 * width + n0 + n_chunk]).astype(BF16)
        gate_ref[:, n0:n0 + n_chunk] = _dot(
            h, wmain_ref[:, 3 * width + n0:3 * width + n0 + n_chunk]).astype(BF16)
    lf_ref[...] = -_softplus(-(_dot(h, wf_ref[...]) + fb_ref[...]))
    _mem_attention(_dot(h, wqm_ref[...]), mqw_ref[...], mk_ref, mv_ref, mo_ref)


def _pad_lanes(v):
    return jnp.zeros((1, LANES), F32).at[0, :v.shape[0]].set(v.astype(F32))


def _pad_cols(w):
    return jnp.zeros((w.shape[0], LANES), w.dtype).at[:, :w.shape[1]].set(w)


def _in_proj_common_specs(tm, d, seq, n_mem, mw):
    assert seq % tm == 0, "a row block must not straddle two sequences"
    steps_per_seq = seq // tm
    x_spec = pl.BlockSpec((tm, d), lambda i: (i, 0))
    mem_spec = pl.BlockSpec((1, n_mem, mw), lambda i: (i // steps_per_seq, 0, 0))
    return x_spec, mem_spec


def _in_proj_dn(x2, seq, norm_w, w_in, a_log, dt_bias, memq_w, mem_k, mem_v, *, tm=512):
    m, d = x2.shape
    heads = a_log.shape[0]
    mix = heads * HEAD_DIM
    n_mem, mw = mem_k.shape[1], mem_k.shape[2]
    w_in = w_in.astype(BF16)
    w_main = w_in[:, :4 * mix]
    w_ab = _pad_cols(w_in[:, 4 * mix:4 * mix + 2 * heads])
    w_qm = w_in[:, 4 * mix + 2 * heads:]
    x_spec, mem_spec = _in_proj_common_specs(tm, d, seq, n_mem, mw)
    row = lambda n: pl.BlockSpec((tm, n), lambda i: (i, 0))
    return pl.pallas_call(
        functools.partial(_in_proj_dn_kernel, n_chunk=512, heads=heads),
        grid=(m // tm,),
        in_specs=[
            x_spec, _const_spec((1, d)), _const_spec((d, 4 * mix)), _const_spec((d, LANES)),
            _const_spec((d, mw)), _const_spec((1, LANES)), _const_spec((1, LANES)),
            _const_spec((1, HEAD_DIM)), mem_spec, mem_spec,
        ],
        out_specs=[row(3 * mix), row(mix), row(LANES), row(mw)],
        out_shape=[
            jax.ShapeDtypeStruct((m, 3 * mix), BF16),
            jax.ShapeDtypeStruct((m, mix), BF16),
            jax.ShapeDtypeStruct((m, LANES), F32),
            jax.ShapeDtypeStruct((m, mw), BF16),
        ],
        compiler_params=pltpu.CompilerParams(
            dimension_semantics=("arbitrary",), vmem_limit_bytes=VMEM_LIMIT),
        name="in_proj_dn",
    )(x2, norm_w.reshape(1, d), w_main, w_ab, w_qm, _pad_lanes(a_log), _pad_lanes(dt_bias),
      memq_w.reshape(1, HEAD_DIM), mem_k, mem_v)


def _in_proj_fox(x2, seq, norm_w, w_in, f_bias, q_norm_w, k_norm_w, memq_w, mem_k, mem_v, *, tm=512):
    m, d = x2.shape
    heads = f_bias.shape[0]
    mix = heads * HEAD_DIM
    n_mem, mw = mem_k.shape[1], mem_k.shape[2]
    w_in = w_in.astype(BF16)
    w_main = w_in[:, :4 * mix]
    w_f = _pad_cols(w_in[:, 4 * mix:4 * mix + heads])
    w_qm = w_in[:, 4 * mix + heads:]
    x_spec, mem_spec = _in_proj_common_specs(tm, d, seq, n_mem, mw)
    row = lambda n: pl.BlockSpec((tm, n), lambda i: (i, 0))
    return pl.pallas_call(
        functools.partial(_in_proj_fox_kernel, n_chunk=512),
        grid=(m // tm,),
        in_specs=[
            x_spec, _const_spec((1, d)), _const_spec((d, 4 * mix)), _const_spec((d, LANES)),
            _const_spec((d, mw)), _const_spec((1, LANES)), _const_spec((1, HEAD_DIM)),
            _const_spec((1, HEAD_DIM)), _const_spec((1, HEAD_DIM)), mem_spec, mem_spec,
        ],
        out_specs=[row(mix), row(mix), row(mix), row(mix), row(LANES), row(mw)],
        out_shape=[jax.ShapeDtypeStruct((m, mix), BF16)] * 4 + [
            jax.ShapeDtypeStruct((m, LANES), F32),
            jax.ShapeDtypeStruct((m, mw), BF16),
        ],
        compiler_params=pltpu.CompilerParams(
            dimension_semantics=("arbitrary",), vmem_limit_bytes=VMEM_LIMIT),
        name="in_proj_fox",
    )(x2, norm_w.reshape(1, d), w_main, w_f, w_qm, _pad_lanes(f_bias),
      q_norm_w.reshape(1, HEAD_DIM), k_norm_w.reshape(1, HEAD_DIM),
      memq_w.reshape(1, HEAD_DIM), mem_k, mem_v)


def _dn_kernel(q_ref, k_ref, v_ref, z_ref, gb_ref, cwq_ref, cwk_ref, cwv_ref, onw_ref, o_ref,
               q_s, k_s, v_s, g_s, b_s, wq_s, u_s, qk_s, kd_s, eg_s, state_s,
               *, hb, chunk, group, heads):
    seq = q_ref.shape[0]
    n_chunks = seq // chunk
    rows = group * chunk
    hg = pl.program_id(1)
    row = lax.broadcasted_iota(jnp.int32, (seq, LANES), 0)
    lane = lax.broadcasted_iota(jnp.int32, (seq, LANES), 1)
    row_in_chunk = jnp.bitwise_and(row, chunk - 1)

    def conv_silu(x, cw):
        acc = x * cw[CONV_WIDTH - 1:CONV_WIDTH, :]
        for s in range(1, CONV_WIDTH):
            xs = jnp.where(row >= s, pltpu.roll(x, s, 0), 0.0)
            acc = acc + xs * cw[CONV_WIDTH - 1 - s:CONV_WIDTH - s, :]
        return acc * _sigmoid(acc)

    def l2n(x):
        return x * lax.rsqrt(jnp.sum(x * x, axis=-1, keepdims=True) + EPS)

    for j in range(hb):
        sl = slice(j * HEAD_DIM, (j + 1) * HEAD_DIM)
        head = hg * hb + j
        q_s[j] = l2n(conv_silu(q_ref[:, sl].astype(F32), cwq_ref[:, sl])) * (HEAD_DIM ** -0.5)
        k_s[j] = l2n(conv_silu(k_ref[:, sl].astype(F32), cwk_ref[:, sl]))
        v_s[j] = conv_silu(v_ref[:, sl].astype(F32), cwv_ref[:, sl])
        gb = gb_ref[...]
        g_s[j] = _chunk_cumsum(_head_column(gb, lane, head), row_in_chunk, chunk)
        b_s[j] = _head_column(gb, lane, heads + head)
        state_s[j] = jnp.zeros((HEAD_DIM, HEAD_DIM), F32)

    ci = lax.broadcasted_iota(jnp.int32, (group, chunk, chunk), 1)
    cj = lax.broadcasted_iota(jnp.int32, (group, chunk, chunk), 2)
    causal = ci >= cj
    strict = ci > cj
    lane3 = lax.broadcasted_iota(jnp.int32, (group, chunk, LANES), 2)
    n_double = chunk.bit_length() - 1

    def bmm(a, b):
        return jnp.einsum("gij,gjk->gik", a, b, preferred_element_type=F32)

    def bmm_nt(a, b):
        return jnp.einsum("gid,gjd->gij", a, b, preferred_element_type=F32)

    def prep(gi, carry):
        r0 = pl.multiple_of(gi * rows, rows)
        c0 = pl.multiple_of(gi * group, group)
        for j in range(hb):
            shape3 = (group, chunk, HEAD_DIM)
            q = q_s[j, pl.ds(r0, rows)].reshape(shape3)
            k = k_s[j, pl.ds(r0, rows)].reshape(shape3)
            v = v_s[j, pl.ds(r0, rows)].reshape(shape3)
            g = g_s[j, pl.ds(r0, rows)].reshape(shape3)
            beta = b_s[j, pl.ds(r0, rows)].reshape(shape3)
            ga, gbm = _diff_operands(g, lane3)
            diff = bmm_nt(ga, gbm)
            decay = jnp.where(causal, jnp.exp(jnp.where(causal, diff, 0.0)), 0.0)
            kb = k * beta
            kq = jnp.concatenate([kb, q], axis=1).astype(BF16)
            aq = bmm_nt(kq, k.astype(BF16))
            p = -jnp.where(strict, aq[:, :chunk] * decay, 0.0)
            qk = aq[:, chunk:] * decay
            eg = jnp.exp(g)
            x = jnp.concatenate([v * beta, kb * eg], axis=2)
            for it in range(n_double):
                pb = p.astype(BF16)
                x = x + bmm(pb, x.astype(BF16))
                if it + 1 < n_double:
                    p = bmm(pb, pb)
            g_last = g[:, chunk - 1:chunk, :]
            wq_s[j, pl.ds(c0, group)] = jnp.concatenate(
                [x[:, :, HEAD_DIM:], q * eg], axis=1).astype(BF16)
            u_s[j, pl.ds(c0, group)] = x[:, :, :HEAD_DIM]
            qk_s[j, pl.ds(c0, group)] = qk.astype(BF16)
            kd_s[j, pl.ds(c0, group)] = (k * jnp.exp(g_last - g)).astype(BF16)
            eg_s[j, pl.ds(c0, group)] = jnp.broadcast_to(jnp.exp(g_last), (group, 8, LANES))
        return carry

    lax.fori_loop(0, n_chunks // group, prep, 0)

    def step(c, carry):
        r0 = pl.multiple_of(c * chunk, chunk)
        for j in range(hb):
            sl = slice(j * HEAD_DIM, (j + 1) * HEAD_DIM)
            state = state_s[j]
            ws = _dot(wq_s[j, c], state.astype(BF16))
            v_new = (u_s[j, c] - ws[:chunk]).astype(BF16)
            out = ws[chunk:] + _dot(qk_s[j, c], v_new)
            state_s[j] = state * eg_s[j, c][0:1, :] + lax.dot_general(
                kd_s[j, c], v_new, (((0,), (0,)), ((), ())), preferred_element_type=F32)
            z = z_ref[pl.ds(r0, chunk), sl].astype(F32)
            o_ref[pl.ds(r0, chunk), sl] = (
                _rms(out, onw_ref[...]) * (z * _sigmoid(z))).astype(BF16)
        return carry

    lax.fori_loop(0, n_chunks, step, 0)


def _deltanet(qkv, z, gb, conv_w, o_norm_w, batch, seq, heads, *, hb=2, group=4):
    m = qkv.shape[0]
    mix = heads * HEAD_DIM
    chunk = DN_CHUNK
    n_chunks = seq // chunk
    hgroups = heads // hb
    wblk = hb * HEAD_DIM
    col = lambda off: pl.BlockSpec((seq, wblk), lambda b, h: (b, off + h))
    cw = lambda off: pl.BlockSpec((CONV_WIDTH, wblk), lambda b, h: (0, off + h))
    return pl.pallas_call(
        functools.partial(_dn_kernel, hb=hb, chunk=chunk, group=group, heads=heads),
        grid=(batch, hgroups),
        in_specs=[
            col(0), col(hgroups), col(2 * hgroups),
            pl.BlockSpec((seq, wblk), lambda b, h: (b, h)),
            pl.BlockSpec((seq, LANES), lambda b, h: (b, 0)),
            cw(0), cw(hgroups), cw(2 * hgroups),
            pl.BlockSpec((1, HEAD_DIM), lambda b, h: (0, 0)),
        ],
        out_specs=pl.BlockSpec((seq, wblk), lambda b, h: (b, h)),
        out_shape=jax.ShapeDtypeStruct((m, mix), BF16),
        scratch_shapes=[
            pltpu.VMEM((hb, seq, HEAD_DIM), F32),
            pltpu.VMEM((hb, seq, HEAD_DIM), F32),
            pltpu.VMEM((hb, seq, HEAD_DIM), F32),
            pltpu.VMEM((hb, seq, LANES), F32),
            pltpu.VMEM((hb, seq, LANES), F32),
            pltpu.VMEM((hb, n_chunks, 2 * chunk, HEAD_DIM), BF16),
            pltpu.VMEM((hb, n_chunks, chunk, HEAD_DIM), F32),
            pltpu.VMEM((hb, n_chunks, chunk, chunk), BF16),
            pltpu.VMEM((hb, n_chunks, chunk, HEAD_DIM), BF16),
            pltpu.VMEM((hb, n_chunks, 8, LANES), F32),
            pltpu.VMEM((hb, HEAD_DIM, HEAD_DIM), F32),
        ],
        compiler_params=pltpu.CompilerParams(
            dimension_semantics=("arbitrary", "arbitrary"), vmem_limit_bytes=VMEM_LIMIT),
        name="deltanet",
    )(qkv, qkv, qkv, z, gb, conv_w, conv_w, conv_w, o_norm_w.reshape(1, HEAD_DIM))


def _fox_kernel(q_ref, k_ref, v_ref, gate_ref, lf_ref, o_ref, qa_s, ka_s, *, tq):
    seq = q_ref.shape[0]
    head = pl.program_id(1)
    row = lax.broadcasted_iota(jnp.int32, (seq, LANES), 0)
    lane = lax.broadcasted_iota(jnp.int32, (seq, LANES), 1)
    f_cum = _chunk_cumsum(_head_column(lf_ref[...], lane, head), row, seq)
    fa, fb = _diff_operands(f_cum, lane)
    qa_s[:, :HEAD_DIM] = q_ref[...]
    qa_s[:, HEAD_DIM:] = fa
    ka_s[:, :HEAD_DIM] = k_ref[...]
    ka_s[:, HEAD_DIM:] = fb
    for qi in range(seq // tq):
        kmax = (qi + 1) * tq
        s = _dot_nt(qa_s[qi * tq:kmax, :], ka_s[:kmax, :])
        q_pos = qi * tq + lax.broadcasted_iota(jnp.int32, (tq, kmax), 0)
        k_pos = lax.broadcasted_iota(jnp.int32, (tq, kmax), 1)
        s = jnp.where(k_pos <= q_pos, s, -jnp.inf)
        p = jnp.exp(s - jnp.max(s, axis=1, keepdims=True))
        o = _dot(p.astype(BF16), v_ref[:kmax, :]) / jnp.sum(p, axis=1, keepdims=True)
        gate = gate_ref[qi * tq:kmax, :].astype(F32)
        o_ref[qi * tq:kmax, :] = (o * _sigmoid(gate)).astype(BF16)


def _fox_attention(q, k, v, gate, lf, batch, seq, heads, *, tq=256):
    m, mix = q.shape
    col = pl.BlockSpec((seq, HEAD_DIM), lambda b, h: (b, h))
    return pl.pallas_call(
        functools.partial(_fox_kernel, tq=tq),
        grid=(batch, heads),
        in_specs=[col, col, col, col, pl.BlockSpec((seq, LANES), lambda b, h: (b, 0))],
        out_specs=col,
        out_shape=jax.ShapeDtypeStruct((m, mix), BF16),
        scratch_shapes=[
            pltpu.VMEM((seq, 2 * HEAD_DIM), BF16),
            pltpu.VMEM((seq, 2 * HEAD_DIM), BF16),
        ],
        compiler_params=pltpu.CompilerParams(
            dimension_semantics=("arbitrary", "arbitrary"), vmem_limit_bytes=VMEM_LIMIT),
        name="fox_attention",
    )(q, k, v, gate, lf)


def _post_kernel(x_ref, mix_ref, mo_ref, wo_ref, nw_ref, w1_ref, w2_ref, y_ref, hid_s, *, f_chunk):
    mixw = mix_ref.shape[1]
    x1 = x_ref[...] + _dot(mix_ref[...], wo_ref[:mixw, :]) + _dot(mo_ref[...], wo_ref[mixw:, :])
    h = _rms(x1, nw_ref[...]).astype(BF16)
    for f0 in range(0, w1_ref.shape[1], f_chunk):
        t = jnp.maximum(_dot(h, w1_ref[:, f0:f0 + f_chunk]), 0.0)
        hid_s[:, f0:f0 + f_chunk] = (t * t).astype(BF16)
    y_ref[...] = x1 + _dot(hid_s[...], w2_ref[...])


def _post(x2, mix, mem_out, w_out, norm_w, w1, w2, *, tm=512, f_chunk=512):
    m, d = x2.shape
    mixw, mw, dff = mix.shape[1], mem_out.shape[1], w1.shape[1]
    row = lambda n: pl.BlockSpec((tm, n), lambda i: (i, 0))
    return pl.pallas_call(
        functools.partial(_post_kernel, f_chunk=f_chunk),
        grid=(m // tm,),
        in_specs=[
            row(d), row(mixw), row(mw), _const_spec((mixw + mw, d)), _const_spec((1, d)),
            _const_spec((d, dff)), _const_spec((dff, d)),
        ],
        out_specs=row(d),
        out_shape=jax.ShapeDtypeStruct((m, d), F32),
        scratch_shapes=[pltpu.VMEM((tm, dff), BF16)],
        compiler_params=pltpu.CompilerParams(
            dimension_semantics=("arbitrary",), vmem_limit_bytes=VMEM_LIMIT),
        name="post",
    )(x2, mix, mem_out, w_out.astype(BF16), norm_w.reshape(1, d), w1.astype(BF16), w2.astype(BF16))


def kernel(x, mem, mem_norm_w, w_mem_kv, mem_k_norm_w, norm1_w, dn_w_in, dn_conv_w, dn_a_log,
           dn_dt_bias, dn_o_norm_w, fox_w_in, fox_f_bias, fox_q_norm_w, fox_k_norm_w, memq_norm_w,
           w_out, norm2_w, w_mlp1, w_mlp2):
    batch, seq, d = x.shape
    depth = norm1_w.shape[0]
    mem_k, mem_v = _mem_kv(mem, mem_norm_w, w_mem_kv, mem_k_norm_w)
    x2 = x.reshape(batch * seq, d)
    for i in range(depth):
        j = i // 2
        if i % 2 == 0:
            heads = dn_a_log.shape[1]
            qkv, z, gb, mem_out = _in_proj_dn(
                x2, seq, norm1_w[i], dn_w_in[j], dn_a_log[j], dn_dt_bias[j], memq_norm_w[i],
                mem_k, mem_v)
            mix = _deltanet(qkv, z, gb, dn_conv_w[j], dn_o_norm_w[j], batch, seq, heads)
        else:
            heads = fox_f_bias.shape[1]
            q, k, v, gate, lf, mem_out = _in_proj_fox(
                x2, seq, norm1_w[i], fox_w_in[j], fox_f_bias[j], fox_q_norm_w[j],
                fox_k_norm_w[j], memq_norm_w[i], mem_k, mem_v)
            mix = _fox_attention(q, k, v, gate, lf, batch, seq, heads)
        x2 = _post(x2, mix, mem_out, w_out[i], norm2_w[i], w_mlp1[i], w_mlp2[i])
    return x2.reshape(batch, seq, d)
```

```python
import functools

import jax
import jax.numpy as jnp
from jax import lax
from jax.experimental import pallas as pl
from jax.experimental.pallas import tpu as pltpu

F32 = jnp.float32
BF16 = jnp.bfloat16

HEAD_DIM = 128
LANES = 128
BF16_ROWS = 16
MEM_HEADS = 4
CONV_WIDTH = 4
EPS = 1e-6
DN_CHUNK = 64
VMEM_LIMIT = 56 * 1024 * 1024
LOG2E = 1.4426950408889634
ONES_LANE = LANES - 1


def _sigmoid(x):
    return 1.0 / (1.0 + jnp.exp(-x))


def _softplus(x):
    return jnp.maximum(x, 0.0) + jnp.log1p(jnp.exp(-jnp.abs(x)))


def _rms(x, w):
    return x * lax.rsqrt(jnp.mean(x * x, axis=-1, keepdims=True) + EPS) * w


def _dot(a, b):
    return jnp.dot(a, b, preferred_element_type=F32)


def _dot_nt(a, b):
    return lax.dot_general(a, b, (((1,), (1,)), ((), ())), preferred_element_type=F32)


def _dot_tn(a, b):
    return lax.dot_general(a, b, (((0,), (0,)), ((), ())), preferred_element_type=F32)


def _split3(x):
    hi = x.astype(BF16).astype(F32)
    r = x - hi
    mid = r.astype(BF16).astype(F32)
    lo = (r - mid).astype(BF16).astype(F32)
    return hi, mid, lo


DIFF_LANES = 8


def _diff_selector(heads):
    assert heads * DIFF_LANES <= LANES and heads <= ONES_LANE
    s = [[0.0] * (2 * LANES) for _ in range(3 * LANES)]
    for h in range(heads):
        for part in range(3):
            s[part * LANES + h][DIFF_LANES * h + part] = 1.0
            s[ONES_LANE][DIFF_LANES * h + 3 + part] = 1.0
            s[ONES_LANE][LANES + DIFF_LANES * h + part] = 1.0
            s[part * LANES + h][LANES + DIFF_LANES * h + 3 + part] = -1.0
    return jnp.array(s, BF16)


def _slabs_with_one(x, lane):
    hi, mid, lo = _split3(x)
    spare = lane == ONES_LANE
    return jnp.concatenate([jnp.where(spare, 1.0, hi), jnp.where(spare, 0.0, mid),
                            jnp.where(spare, 0.0, lo)], axis=1).astype(BF16)


def _chunk_cumsum(x, row_in_chunk, length):
    s = 1
    while s < length:
        x = x + jnp.where(row_in_chunk >= s, pltpu.roll(x, s, 0), 0.0)
        s *= 2
    return x


def _const_spec(shape):
    return pl.BlockSpec(shape, lambda *_: (0,) * len(shape), pipeline_mode=pl.Buffered(1))


def _mem_kv_kernel(mem_ref, nw_ref, w_ref, knw_ref, k_ref, v_ref):
    h = _rms(mem_ref[0], nw_ref[...]).astype(BF16)
    kv = _dot(h, w_ref[...])
    mw = kv.shape[1] // 2
    for hd in range(mw // HEAD_DIM):
        sl = slice(hd * HEAD_DIM, (hd + 1) * HEAD_DIM)
        k_ref[0, :, sl] = _rms(kv[:, sl], knw_ref[...]).astype(BF16)
    v_ref[0] = kv[:, mw:].astype(BF16)


def _mem_kv(mem, mem_norm_w, w_mem_kv, mem_k_norm_w):
    b, n_mem, d = mem.shape
    mw = w_mem_kv.shape[1] // 2
    return pl.pallas_call(
        _mem_kv_kernel,
        grid=(b,),
        in_specs=[
            pl.BlockSpec((1, n_mem, d), lambda i: (i, 0, 0)),
            _const_spec((1, d)),
            _const_spec((d, 2 * mw)),
            _const_spec((1, HEAD_DIM)),
        ],
        out_specs=[
            pl.BlockSpec((1, n_mem, mw), lambda i: (i, 0, 0)),
            pl.BlockSpec((1, n_mem, mw), lambda i: (i, 0, 0)),
        ],
        out_shape=[jax.ShapeDtypeStruct((b, n_mem, mw), BF16)] * 2,
        compiler_params=pltpu.CompilerParams(
            dimension_semantics=("arbitrary",), vmem_limit_bytes=VMEM_LIMIT),
        name="mem_kv",
    )(mem, mem_norm_w.reshape(1, d), w_mem_kv.astype(BF16), mem_k_norm_w.reshape(1, HEAD_DIM))


def _mem_attention(qm, mqw, mk_ref, mv_ref, mo_ref):
    scale = HEAD_DIM ** -0.5
    for hd in range(MEM_HEADS):
        sl = slice(hd * HEAD_DIM, (hd + 1) * HEAD_DIM)
        q = (_rms(qm[:, sl], mqw) * scale).astype(BF16)
        s = _dot_nt(q, mk_ref[0, :, sl])
        p = jnp.exp(s - jnp.max(s, axis=1, keepdims=True))
        o = _dot(p.astype(BF16), mv_ref[0, :, sl]) / jnp.sum(p, axis=1, keepdims=True)
        mo_ref[:, sl] = o.astype(BF16)


def _in_proj_dn_kernel(x_ref, nw_ref, wmain_ref, wab_ref, wqm_ref, alog_ref, dtb_ref, mqw_ref,
                       sel_ref, mk_ref, mv_ref, qkv_ref, z_ref, gb3_ref, gab_ref, mo_ref,
                       *, n_chunk, heads, chunk):
    h = _rms(x_ref[...], nw_ref[...]).astype(BF16)
    qkv_w = qkv_ref.shape[1]
    for n0 in range(0, qkv_w, n_chunk):
        qkv_ref[:, n0:n0 + n_chunk] = _dot(h, wmain_ref[:, n0:n0 + n_chunk]).astype(BF16)
    for n0 in range(0, z_ref.shape[1], n_chunk):
        z_ref[:, n0:n0 + n_chunk] = _dot(
            h, wmain_ref[:, qkv_w + n0:qkv_w + n0 + n_chunk]).astype(BF16)
    ab = _dot(h, wab_ref[...])
    lane = lax.broadcasted_iota(jnp.int32, ab.shape, 1)
    row = lax.broadcasted_iota(jnp.int32, ab.shape, 0)
    g = -jnp.exp(alog_ref[...]) * _softplus(ab + dtb_ref[...])
    g_cum = _chunk_cumsum(g, jnp.bitwise_and(row, chunk - 1), chunk)
    slabs = _slabs_with_one(jnp.where(lane < heads, g_cum, _sigmoid(ab)), lane)
    gb3_ref[...] = slabs
    gab_ref[...] = _dot(slabs, sel_ref[...]).astype(BF16)
    _mem_attention(_dot(h, wqm_ref[...]), mqw_ref[...], mk_ref, mv_ref, mo_ref)


def _in_proj_fox_kernel(x_ref, nw_ref, wmain_ref, wf_ref, wqm_ref, fb_ref, qnw_ref, knw_ref,
                        mqw_ref, sel_ref, mk_ref, mv_ref, q_ref, k_ref, v_ref, gate_ref, fab_ref,
                        mo_ref, carry_s, *, n_chunk, steps_per_seq):
    h = _rms(x_ref[...], nw_ref[...]).astype(BF16)
    width = q_ref.shape[1]
    scale = HEAD_DIM ** -0.5 * LOG2E
    for n0 in range(0, width, n_chunk):
        yq = _dot(h, wmain_ref[:, n0:n0 + n_chunk])
        yk = _dot(h, wmain_ref[:, width + n0:width + n0 + n_chunk])
        for c0 in range(0, n_chunk, HEAD_DIM):
            q_ref[:, n0 + c0:n0 + c0 + HEAD_DIM] = (
                _rms(yq[:, c0:c0 + HEAD_DIM], qnw_ref[...]) * scale).astype(BF16)
            k_ref[:, n0 + c0:n0 + c0 + HEAD_DIM] = _rms(
                yk[:, c0:c0 + HEAD_DIM], knw_ref[...]).astype(BF16)
        v_ref[:, n0:n0 + n_chunk] = _dot(
            h, wmain_ref[:, 2 * width + n0:2 * width + n0 + n_chunk]).astype(BF16)
        gate_ref[:, n0:n0 + n_chunk] = _dot(
            h, wmain_ref[:, 3 * width + n0:3 * width + n0 + n_chunk]).astype(BF16)

    @pl.when(pl.program_id(0) % steps_per_seq == 0)
    def _():
        carry_s[...] = jnp.zeros_like(carry_s)

    lf = -_softplus(-(_dot(h, wf_ref[...]) + fb_ref[...])) * LOG2E
    tm = lf.shape[0]
    row = lax.broadcasted_iota(jnp.int32, lf.shape, 0)
    lane = lax.broadcasted_iota(jnp.int32, lf.shape, 1)
    f_cum = _chunk_cumsum(lf, row, tm) + carry_s[0:1, :]
    carry_s[...] = jnp.broadcast_to(f_cum[tm - 1:tm, :], carry_s.shape)
    fab_ref[...] = _dot(_slabs_with_one(f_cum, lane), sel_ref[...]).astype(BF16)
    _mem_attention(_dot(h, wqm_ref[...]), mqw_ref[...], mk_ref, mv_ref, mo_ref)


def _pad_lanes(v):
    return jnp.zeros((1, LANES), F32).at[0, :v.shape[0]].set(v.astype(F32))


def _pad_cols(w):
    return jnp.zeros((w.shape[0], LANES), w.dtype).at[:, :w.shape[1]].set(w)


def _in_proj_common_specs(tm, d, seq, n_mem, mw):
    assert seq % tm == 0, "a row block must not straddle two sequences"
    steps_per_seq = seq // tm
    x_spec = pl.BlockSpec((tm, d), lambda i: (i, 0))
    mem_spec = pl.BlockSpec((1, n_mem, mw), lambda i: (i // steps_per_seq, 0, 0))
    return x_spec, mem_spec


def _in_proj_dn(x2, seq, norm_w, w_in, a_log, dt_bias, memq_w, mem_k, mem_v, *, tm=512):
    m, d = x2.shape
    heads = a_log.shape[0]
    mix = heads * HEAD_DIM
    n_mem, mw = mem_k.shape[1], mem_k.shape[2]
    w_in = w_in.astype(BF16)
    w_main = w_in[:, :4 * mix]
    w_ab = _pad_cols(w_in[:, 4 * mix:4 * mix + 2 * heads])
    w_qm = w_in[:, 4 * mix + 2 * heads:]
    x_spec, mem_spec = _in_proj_common_specs(tm, d, seq, n_mem, mw)
    row = lambda n: pl.BlockSpec((tm, n), lambda i: (i, 0))
    return pl.pallas_call(
        functools.partial(_in_proj_dn_kernel, n_chunk=512, heads=heads, chunk=DN_CHUNK),
        grid=(m // tm,),
        in_specs=[
            x_spec, _const_spec((1, d)), _const_spec((d, 4 * mix)), _const_spec((d, LANES)),
            _const_spec((d, mw)), _const_spec((1, LANES)), _const_spec((1, LANES)),
            _const_spec((1, HEAD_DIM)), _const_spec((3 * LANES, 2 * LANES)), mem_spec, mem_spec,
        ],
        out_specs=[row(3 * mix), row(mix), row(3 * LANES), row(2 * LANES), row(mw)],
        out_shape=[
            jax.ShapeDtypeStruct((m, 3 * mix), BF16),
            jax.ShapeDtypeStruct((m, mix), BF16),
            jax.ShapeDtypeStruct((m, 3 * LANES), BF16),
            jax.ShapeDtypeStruct((m, 2 * LANES), BF16),
            jax.ShapeDtypeStruct((m, mw), BF16),
        ],
        compiler_params=pltpu.CompilerParams(
            dimension_semantics=("arbitrary",), vmem_limit_bytes=VMEM_LIMIT),
        name="in_proj_dn",
    )(x2, norm_w.reshape(1, d), w_main, w_ab, w_qm, _pad_lanes(a_log), _pad_lanes(dt_bias),
      memq_w.reshape(1, HEAD_DIM), _diff_selector(heads), mem_k, mem_v)


def _in_proj_fox(x2, seq, norm_w, w_in, f_bias, q_norm_w, k_norm_w, memq_w, mem_k, mem_v, *, tm=512):
    m, d = x2.shape
    heads = f_bias.shape[0]
    mix = heads * HEAD_DIM
    n_mem, mw = mem_k.shape[1], mem_k.shape[2]
    w_in = w_in.astype(BF16)
    w_main = w_in[:, :4 * mix]
    w_f = _pad_cols(w_in[:, 4 * mix:4 * mix + heads])
    w_qm = w_in[:, 4 * mix + heads:]
    x_spec, mem_spec = _in_proj_common_specs(tm, d, seq, n_mem, mw)
    row = lambda n: pl.BlockSpec((tm, n), lambda i: (i, 0))
    return pl.pallas_call(
        functools.partial(_in_proj_fox_kernel, n_chunk=512, steps_per_seq=seq // tm),
        grid=(m // tm,),
        in_specs=[
            x_spec, _const_spec((1, d)), _const_spec((d, 4 * mix)), _const_spec((d, LANES)),
            _const_spec((d, mw)), _const_spec((1, LANES)), _const_spec((1, HEAD_DIM)),
            _const_spec((1, HEAD_DIM)), _const_spec((1, HEAD_DIM)),
            _const_spec((3 * LANES, 2 * LANES)), mem_spec, mem_spec,
        ],
        out_specs=[row(mix), row(mix), row(mix), row(mix), row(2 * LANES), row(mw)],
        out_shape=[jax.ShapeDtypeStruct((m, mix), BF16)] * 4 + [
            jax.ShapeDtypeStruct((m, 2 * LANES), BF16),
            jax.ShapeDtypeStruct((m, mw), BF16),
        ],
        scratch_shapes=[pltpu.VMEM((8, LANES), F32)],
        compiler_params=pltpu.CompilerParams(
            dimension_semantics=("arbitrary",), vmem_limit_bytes=VMEM_LIMIT),
        name="in_proj_fox",
    )(x2, norm_w.reshape(1, d), w_main, w_f, w_qm, _pad_lanes(f_bias),
      q_norm_w.reshape(1, HEAD_DIM), k_norm_w.reshape(1, HEAD_DIM),
      memq_w.reshape(1, HEAD_DIM), _diff_selector(heads), mem_k, mem_v)


def _dn_kernel(q_ref, k_ref, v_ref, z_ref, gb3_ref, gab_ref, cwq_ref, cwk_ref, cwv_ref, onw_ref,
               o_ref, e_s, kc_s, nc_s, qp_s, op_s, eg_s, state_s, *, hb, chunk, group, heads):
    seq = q_ref.shape[0]
    n_chunks = seq // chunk
    rows = group * chunk
    hg = pl.program_id(1)

    er = lax.broadcasted_iota(jnp.int32, (3 * LANES, 2 * LANES), 0)
    ec = lax.broadcasted_iota(jnp.int32, (3 * LANES, 2 * LANES), 1)
    for j in range(hb):
        head = hg * hb + j
        want = jnp.where(ec < LANES, head, heads + head)
        e_s[j] = jnp.where(jnp.bitwise_and(er, LANES - 1) == want, 1.0, 0.0).astype(BF16)
        state_s[j] = jnp.zeros((HEAD_DIM, HEAD_DIM), F32)

    def conv_silu(ref, cw_ref, sl, r0, first):
        main = ref[pl.ds(r0, rows), sl].astype(F32)
        h0 = pl.multiple_of(jnp.maximum(r0 - BF16_ROWS, 0), BF16_ROWS)
        halo = jnp.where(first, 0.0, ref[pl.ds(h0, BF16_ROWS), sl].astype(F32))
        ext = jnp.concatenate([halo, main], axis=0)
        cw = cw_ref[:, sl]
        acc = main * cw[CONV_WIDTH - 1:CONV_WIDTH, :]
        for s in range(1, CONV_WIDTH):
            acc = acc + pltpu.roll(ext, s, 0)[BF16_ROWS:] * cw[CONV_WIDTH - 1 - s:CONV_WIDTH - s, :]
        return acc * _sigmoid(acc)

    def l2n(x):
        return x * lax.rsqrt(jnp.sum(x * x, axis=-1, keepdims=True) + EPS)

    assert 2 * chunk == LANES
    ci = lax.broadcasted_iota(jnp.int32, (group, chunk, LANES), 1)
    cj = lax.broadcasted_iota(jnp.int32, (group, chunk, LANES), 2)
    causal = ci >= cj
    strict = ci > cj
    left = cj < chunk
    eye_right = jnp.where(cj - chunk == ci, 1.0, 0.0)
    lane2 = lax.broadcasted_iota(jnp.int32, (rows, LANES), 1)
    zeros_rows = jnp.zeros((group, chunk, LANES), BF16)
    zeros_wide = jnp.zeros((group, chunk, 2 * HEAD_DIM), BF16)
    n_double = chunk.bit_length() - 1
    shape3 = (group, chunk, HEAD_DIM)

    def bmm(a, b):
        return jnp.einsum("gij,gjk->gik", a, b, preferred_element_type=F32)

    def bmm_nt(a, b):
        return jnp.einsum("gid,gjd->gij", a, b, preferred_element_type=F32)

    def prep(gi, carry):
        r0 = pl.multiple_of(gi * rows, rows)
        c0 = pl.multiple_of(gi * group, group)
        first = gi == 0
        for j in range(hb):
            sl = slice(j * HEAD_DIM, (j + 1) * HEAD_DIM)
            q = (l2n(conv_silu(q_ref, cwq_ref, sl, r0, first)) * (HEAD_DIM ** -0.5)).reshape(shape3)
            k = l2n(conv_silu(k_ref, cwk_ref, sl, r0, first)).reshape(shape3)
            v = conv_silu(v_ref, cwv_ref, sl, r0, first).reshape(shape3)
            gbb = _dot(gb3_ref[pl.ds(r0, rows), :], e_s[j])
            g = gbb[:, :LANES].reshape(shape3)
            beta = gbb[:, LANES:].reshape(shape3)
            gab = gab_ref[pl.ds(r0, rows), :]
            own = jnp.right_shift(lane2, 3) == hg * hb + j
            ga = gab[:, :LANES].reshape(shape3)
            gbm = jnp.where(own, gab[:, LANES:], jnp.zeros_like(gab[:, LANES:])).reshape(shape3)
            diff = bmm_nt(ga, jnp.concatenate([gbm, zeros_rows], axis=1))
            decay = jnp.where(causal, jnp.exp(jnp.where(causal, diff, 0.0)), 0.0)
            kb = k * beta
            kq = jnp.concatenate([kb, q], axis=1).astype(BF16)
            aq = bmm_nt(kq, jnp.concatenate([k.astype(BF16), zeros_rows], axis=1))
            qk = (aq[:, chunk:] * decay)[:, :, :chunk].astype(BF16)
            y = eye_right - jnp.where(strict, aq[:, :chunk] * decay, 0.0)
            for _ in range(n_double):
                yb = y.astype(BF16)
                py = bmm(yb, jnp.concatenate([yb, zeros_rows], axis=1))
                y = jnp.where(left, py, y + py)
            eg = jnp.exp(g)
            rhs = jnp.concatenate([v * beta, kb * eg], axis=2).astype(BF16)
            x = bmm(y.astype(BF16), jnp.concatenate([zeros_wide, rhs], axis=1))
            xb = x.astype(BF16)
            qx = bmm(qk, xb)
            g_last = g[:, chunk - 1:chunk, :]
            kd = (k * jnp.exp(g_last - g)).astype(BF16)
            for c in range(group):
                kx = _dot_tn(kd[c], xb[c])
                nc_s[j, c0 + c] = kx[:, :HEAD_DIM]
                kc_s[j, c0 + c] = kx[:, HEAD_DIM:].astype(BF16)
            qp_s[j, pl.ds(c0, group)] = (q * eg - qx[:, :, HEAD_DIM:]).astype(BF16)
            op_s[j, pl.ds(c0, group)] = qx[:, :, :HEAD_DIM]
            eg_s[j, pl.ds(c0, group)] = jnp.broadcast_to(jnp.exp(g_last), (group, 8, LANES))
        return carry

    lax.fori_loop(0, n_chunks // group, prep, 0)

    def step(c, carry):
        r0 = pl.multiple_of(c * chunk, chunk)
        for j in range(hb):
            sl = slice(j * HEAD_DIM, (j + 1) * HEAD_DIM)
            state = state_s[j]
            sb = state.astype(BF16)
            out = _dot(qp_s[j, c], sb) + op_s[j, c]
            state_s[j] = state * eg_s[j, c][0:1, :] + nc_s[j, c] - _dot(kc_s[j, c], sb)
            z = z_ref[pl.ds(r0, chunk), sl].astype(F32)
            o_ref[pl.ds(r0, chunk), sl] = (
                _rms(out, onw_ref[...]) * (z * _sigmoid(z))).astype(BF16)
        return carry

    lax.fori_loop(0, n_chunks, step, 0)


def _deltanet(qkv, z, gb3, gab, conv_w, o_norm_w, batch, seq, heads, *, hb=4, group=16):
    m = qkv.shape[0]
    mix = heads * HEAD_DIM
    chunk = DN_CHUNK
    group = min(group, seq // chunk)
    n_chunks = seq // chunk
    hgroups = heads // hb
    wblk = hb * HEAD_DIM
    col = lambda off: pl.BlockSpec((seq, wblk), lambda b, h: (b, off + h))
    cw = lambda off: pl.BlockSpec((CONV_WIDTH, wblk), lambda b, h: (0, off + h))
    return pl.pallas_call(
        functools.partial(_dn_kernel, hb=hb, chunk=chunk, group=group, heads=heads),
        grid=(batch, hgroups),
        in_specs=[
            col(0), col(hgroups), col(2 * hgroups),
            pl.BlockSpec((seq, wblk), lambda b, h: (b, h)),
            pl.BlockSpec((seq, 3 * LANES), lambda b, h: (b, 0)),
            pl.BlockSpec((seq, 2 * LANES), lambda b, h: (b, 0)),
            cw(0), cw(hgroups), cw(2 * hgroups),
            pl.BlockSpec((1, HEAD_DIM), lambda b, h: (0, 0)),
        ],
        out_specs=pl.BlockSpec((seq, wblk), lambda b, h: (b, h)),
        out_shape=jax.ShapeDtypeStruct((m, mix), BF16),
        scratch_shapes=[
            pltpu.VMEM((hb, 3 * LANES, 2 * LANES), BF16),
            pltpu.VMEM((hb, n_chunks, HEAD_DIM, HEAD_DIM), BF16),
            pltpu.VMEM((hb, n_chunks, HEAD_DIM, HEAD_DIM), F32),
            pltpu.VMEM((hb, n_chunks, chunk, HEAD_DIM), BF16),
            pltpu.VMEM((hb, n_chunks, chunk, HEAD_DIM), F32),
            pltpu.VMEM((hb, n_chunks, 8, LANES), F32),
            pltpu.VMEM((hb, HEAD_DIM, HEAD_DIM), F32),
        ],
        compiler_params=pltpu.CompilerParams(
            dimension_semantics=("arbitrary", "arbitrary"), vmem_limit_bytes=VMEM_LIMIT),
        name="deltanet",
    )(qkv, qkv, qkv, z, gb3, gab, conv_w, conv_w, conv_w, o_norm_w.reshape(1, HEAD_DIM))


def _fox_kernel(q_ref, k_ref, v_ref, gate_ref, fab_ref, o_ref, qa_s, ka_s, *, tq):
    seq = q_ref.shape[0]
    head = pl.program_id(1)
    lane = lax.broadcasted_iota(jnp.int32, (seq, LANES), 1)
    fb = fab_ref[:, LANES:]
    qa_s[:, :HEAD_DIM] = q_ref[...]
    qa_s[:, HEAD_DIM:] = fab_ref[:, :LANES]
    ka_s[:, :HEAD_DIM] = k_ref[...]
    ka_s[:, HEAD_DIM:] = jnp.where(jnp.right_shift(lane, 3) == head, fb, jnp.zeros_like(fb))

    on_or_below = (lax.broadcasted_iota(jnp.int32, (tq, tq), 1)
                   <= lax.broadcasted_iota(jnp.int32, (tq, tq), 0))

    def scores(qi):
        lo, hi = qi * tq, (qi + 1) * tq
        qa = qa_s[lo:hi, :]
        sd = jnp.where(on_or_below, _dot_nt(qa, ka_s[lo:hi, :]), -jnp.inf)
        so = _dot_nt(qa, ka_s[:lo, :]) if qi > 0 else None
        return sd, so

    n_q = seq // tq
    nxt = scores(0)
    for qi in range(n_q):
        lo, hi = qi * tq, (qi + 1) * tq
        sd, so = nxt
        if qi + 1 < n_q:
            nxt = scores(qi + 1)
        m = jnp.max(sd, axis=1, keepdims=True)
        if so is not None:
            m = jnp.maximum(m, jnp.max(so, axis=1, keepdims=True))
        pd = jnp.exp2(sd - m)
        l = jnp.sum(pd, axis=1, keepdims=True)
        o = _dot(pd.astype(BF16), v_ref[lo:hi, :])
        if so is not None:
            po = jnp.exp2(so - m)
            l = l + jnp.sum(po, axis=1, keepdims=True)
            o = o + _dot(po.astype(BF16), v_ref[:lo, :])
        gate = gate_ref[lo:hi, :].astype(F32)
        o_ref[lo:hi, :] = (o / l * _sigmoid(gate)).astype(BF16)


def _fox_attention(q, k, v, gate, fab, batch, seq, heads, *, tq=256):
    m, mix = q.shape
    tq = min(tq, seq)
    col = pl.BlockSpec((seq, HEAD_DIM), lambda b, h: (b, h))
    return pl.pallas_call(
        functools.partial(_fox_kernel, tq=tq),
        grid=(batch, heads),
        in_specs=[col, col, col, col, pl.BlockSpec((seq, 2 * LANES), lambda b, h: (b, 0))],
        out_specs=col,
        out_shape=jax.ShapeDtypeStruct((m, mix), BF16),
        scratch_shapes=[
            pltpu.VMEM((seq, 2 * HEAD_DIM), BF16),
            pltpu.VMEM((seq, 2 * HEAD_DIM), BF16),
        ],
        compiler_params=pltpu.CompilerParams(
            dimension_semantics=("arbitrary", "arbitrary"), vmem_limit_bytes=VMEM_LIMIT),
        name="fox_attention",
    )(q, k, v, gate, fab)


def _post_kernel(x_ref, mix_ref, mo_ref, wo_ref, nw_ref, w1_ref, w2_ref, y_ref, hid_s, *, f_chunk):
    mixw = mix_ref.shape[1]
    x1 = x_ref[...] + _dot(mix_ref[...], wo_ref[:mixw, :]) + _dot(mo_ref[...], wo_ref[mixw:, :])
    h = _rms(x1, nw_ref[...]).astype(BF16)
    for f0 in range(0, w1_ref.shape[1], f_chunk):
        t = jnp.maximum(_dot(h, w1_ref[:, f0:f0 + f_chunk]), 0.0)
        hid_s[:, f0:f0 + f_chunk] = (t * t).astype(BF16)
    y_ref[...] = x1 + _dot(hid_s[...], w2_ref[...])


def _post(x2, mix, mem_out, w_out, norm_w, w1, w2, *, tm=512, f_chunk=512):
    m, d = x2.shape
    mixw, mw, dff = mix.shape[1], mem_out.shape[1], w1.shape[1]
    row = lambda n: pl.BlockSpec((tm, n), lambda i: (i, 0))
    return pl.pallas_call(
        functools.partial(_post_kernel, f_chunk=f_chunk),
        grid=(m // tm,),
        in_specs=[
            row(d), row(mixw), row(mw), _const_spec((mixw + mw, d)), _const_spec((1, d)),
            _const_spec((d, dff)), _const_spec((dff, d)),
        ],
        out_specs=row(d),
        out_shape=jax.ShapeDtypeStruct((m, d), F32),
        scratch_shapes=[pltpu.VMEM((tm, dff), BF16)],
        compiler_params=pltpu.CompilerParams(
            dimension_semantics=("arbitrary",), vmem_limit_bytes=VMEM_LIMIT),
        name="post",
    )(x2, mix, mem_out, w_out.astype(BF16), norm_w.reshape(1, d), w1.astype(BF16), w2.astype(BF16))


def kernel(x, mem, mem_norm_w, w_mem_kv, mem_k_norm_w, norm1_w, dn_w_in, dn_conv_w, dn_a_log,
           dn_dt_bias, dn_o_norm_w, fox_w_in, fox_f_bias, fox_q_norm_w, fox_k_norm_w, memq_norm_w,
           w_out, norm2_w, w_mlp1, w_mlp2):
    batch, seq, d = x.shape
    depth = norm1_w.shape[0]
    mem_k, mem_v = _mem_kv(mem, mem_norm_w, w_mem_kv, mem_k_norm_w)
    x2 = x.reshape(batch * seq, d)
    for i in range(depth):
        j = i // 2
        if i % 2 == 0:
            heads = dn_a_log.shape[1]
            qkv, z, gb3, gab, mem_out = _in_proj_dn(
                x2, seq, norm1_w[i], dn_w_in[j], dn_a_log[j], dn_dt_bias[j], memq_norm_w[i],
                mem_k, mem_v)
            mix = _deltanet(qkv, z, gb3, gab, dn_conv_w[j], dn_o_norm_w[j], batch, seq, heads)
        else:
            heads = fox_f_bias.shape[1]
            q, k, v, gate, fab, mem_out = _in_proj_fox(
                x2, seq, norm1_w[i], fox_w_in[j], fox_f_bias[j], fox_q_norm_w[j],
                fox_k_norm_w[j], memq_norm_w[i], mem_k, mem_v)
            mix = _fox_attention(q, k, v, gate, fab, batch, seq, heads)
        x2 = _post(x2, mix, mem_out, w_out[i], norm2_w[i], w_mlp1[i], w_mlp2[i])
    return x2.reshape(batch, seq, d)
```

```python
import functools

import jax
import jax.numpy as jnp
from jax import lax
from jax.experimental import pallas as pl
from jax.experimental.pallas import tpu as pltpu

F32 = jnp.float32
BF16 = jnp.bfloat16

HEAD_DIM = 128
LANES = 128
BF16_ROWS = 16
MEM_HEADS = 4
CONV_WIDTH = 4
EPS = 1e-6
DN_CHUNK = 64
STEP_UNROLL = 4
VMEM_LIMIT = 56 * 1024 * 1024
LOG2E = 1.4426950408889634
ONES_LANE = LANES - 1


def _sigmoid(x):
    return 1.0 / (1.0 + jnp.exp(-x))


def _softplus(x):
    return jnp.maximum(x, 0.0) + jnp.log1p(jnp.exp(-jnp.abs(x)))


def _rms(x, w):
    return x * lax.rsqrt(jnp.mean(x * x, axis=-1, keepdims=True) + EPS) * w


def _dot(a, b):
    return jnp.dot(a, b, preferred_element_type=F32)


def _dot_nt(a, b):
    return lax.dot_general(a, b, (((1,), (1,)), ((), ())), preferred_element_type=F32)


def _dot_tn(a, b):
    return lax.dot_general(a, b, (((0,), (0,)), ((), ())), preferred_element_type=F32)


def _split3(x):
    hi = x.astype(BF16).astype(F32)
    r = x - hi
    mid = r.astype(BF16).astype(F32)
    lo = (r - mid).astype(BF16).astype(F32)
    return hi, mid, lo


DIFF_LANES = 8


def _diff_selector(heads):
    assert heads * DIFF_LANES <= LANES and heads <= ONES_LANE
    s = [[0.0] * (2 * LANES) for _ in range(3 * LANES)]
    for h in range(heads):
        for part in range(3):
            s[part * LANES + h][DIFF_LANES * h + part] = 1.0
            s[ONES_LANE][DIFF_LANES * h + 3 + part] = 1.0
            s[ONES_LANE][LANES + DIFF_LANES * h + part] = 1.0
            s[part * LANES + h][LANES + DIFF_LANES * h + 3 + part] = -1.0
    return jnp.array(s, BF16)


def _slabs_with_one(x, lane):
    hi, mid, lo = _split3(x)
    spare = lane == ONES_LANE
    return jnp.concatenate([jnp.where(spare, 1.0, hi), jnp.where(spare, 0.0, mid),
                            jnp.where(spare, 0.0, lo)], axis=1).astype(BF16)


def _chunk_cumsum(x, row_in_chunk, length):
    s = 1
    while s < length:
        x = x + jnp.where(row_in_chunk >= s, pltpu.roll(x, s, 0), 0.0)
        s *= 2
    return x


def _const_spec(shape):
    return pl.BlockSpec(shape, lambda *_: (0,) * len(shape), pipeline_mode=pl.Buffered(1))


def _mem_kv_kernel(mem_ref, nw_ref, w_ref, knw_ref, k_ref, v_ref):
    h = _rms(mem_ref[0], nw_ref[...]).astype(BF16)
    kv = _dot(h, w_ref[...])
    mw = kv.shape[1] // 2
    for hd in range(mw // HEAD_DIM):
        sl = slice(hd * HEAD_DIM, (hd + 1) * HEAD_DIM)
        k_ref[0, :, sl] = _rms(kv[:, sl], knw_ref[...]).astype(BF16)
    v_ref[0] = kv[:, mw:].astype(BF16)


def _mem_kv(mem, mem_norm_w, w_mem_kv, mem_k_norm_w):
    b, n_mem, d = mem.shape
    mw = w_mem_kv.shape[1] // 2
    return pl.pallas_call(
        _mem_kv_kernel,
        grid=(b,),
        in_specs=[
            pl.BlockSpec((1, n_mem, d), lambda i: (i, 0, 0)),
            _const_spec((1, d)),
            _const_spec((d, 2 * mw)),
            _const_spec((1, HEAD_DIM)),
        ],
        out_specs=[
            pl.BlockSpec((1, n_mem, mw), lambda i: (i, 0, 0)),
            pl.BlockSpec((1, n_mem, mw), lambda i: (i, 0, 0)),
        ],
        out_shape=[jax.ShapeDtypeStruct((b, n_mem, mw), BF16)] * 2,
        compiler_params=pltpu.CompilerParams(
            dimension_semantics=("arbitrary",), vmem_limit_bytes=VMEM_LIMIT),
        name="mem_kv",
    )(mem, mem_norm_w.reshape(1, d), w_mem_kv.astype(BF16), mem_k_norm_w.reshape(1, HEAD_DIM))


def _mem_attention(qm, mqw, mk_ref, mv_ref, mo_ref):
    scale = HEAD_DIM ** -0.5
    for hd in range(MEM_HEADS):
        sl = slice(hd * HEAD_DIM, (hd + 1) * HEAD_DIM)
        q = (_rms(qm[:, sl], mqw) * scale).astype(BF16)
        s = _dot_nt(q, mk_ref[0, :, sl])
        p = jnp.exp(s - jnp.max(s, axis=1, keepdims=True))
        o = _dot(p.astype(BF16), mv_ref[0, :, sl]) / jnp.sum(p, axis=1, keepdims=True)
        mo_ref[:, sl] = o.astype(BF16)


def _in_proj_dn_kernel(x_ref, nw_ref, wmain_ref, wab_ref, wqm_ref, alog_ref, dtb_ref, mqw_ref,
                       sel_ref, mk_ref, mv_ref, qkv_ref, z_ref, gb3_ref, gab_ref, mo_ref,
                       *, n_chunk, heads, chunk):
    h = _rms(x_ref[...], nw_ref[...]).astype(BF16)
    qkv_w = qkv_ref.shape[1]
    for n0 in range(0, qkv_w, n_chunk):
        qkv_ref[:, n0:n0 + n_chunk] = _dot(h, wmain_ref[:, n0:n0 + n_chunk]).astype(BF16)
    for n0 in range(0, z_ref.shape[1], n_chunk):
        z_ref[:, n0:n0 + n_chunk] = _dot(
            h, wmain_ref[:, qkv_w + n0:qkv_w + n0 + n_chunk]).astype(BF16)
    ab = _dot(h, wab_ref[...])
    lane = lax.broadcasted_iota(jnp.int32, ab.shape, 1)
    row = lax.broadcasted_iota(jnp.int32, ab.shape, 0)
    g = -jnp.exp(alog_ref[...]) * _softplus(ab + dtb_ref[...])
    g_cum = _chunk_cumsum(g, jnp.bitwise_and(row, chunk - 1), chunk)
    slabs = _slabs_with_one(jnp.where(lane < heads, g_cum, _sigmoid(ab)), lane)
    gb3_ref[...] = slabs
    gab_ref[...] = _dot(slabs, sel_ref[...]).astype(BF16)
    _mem_attention(_dot(h, wqm_ref[...]), mqw_ref[...], mk_ref, mv_ref, mo_ref)


def _in_proj_fox_kernel(x_ref, nw_ref, wmain_ref, wf_ref, wqm_ref, fb_ref, qnw_ref, knw_ref,
                        mqw_ref, sel_ref, mk_ref, mv_ref, q_ref, k_ref, v_ref, gate_ref, fab_ref,
                        mo_ref, carry_s, *, n_chunk, steps_per_seq):
    h = _rms(x_ref[...], nw_ref[...]).astype(BF16)
    width = q_ref.shape[1]
    scale = HEAD_DIM ** -0.5 * LOG2E
    for n0 in range(0, width, n_chunk):
        yq = _dot(h, wmain_ref[:, n0:n0 + n_chunk])
        yk = _dot(h, wmain_ref[:, width + n0:width + n0 + n_chunk])
        for c0 in range(0, n_chunk, HEAD_DIM):
            q_ref[:, n0 + c0:n0 + c0 + HEAD_DIM] = (
                _rms(yq[:, c0:c0 + HEAD_DIM], qnw_ref[...]) * scale).astype(BF16)
            k_ref[:, n0 + c0:n0 + c0 + HEAD_DIM] = _rms(
                yk[:, c0:c0 + HEAD_DIM], knw_ref[...]).astype(BF16)
        v_ref[:, n0:n0 + n_chunk] = _dot(
            h, wmain_ref[:, 2 * width + n0:2 * width + n0 + n_chunk]).astype(BF16)
        gate_ref[:, n0:n0 + n_chunk] = _dot(
            h, wmain_ref[:, 3 * width + n0:3 * width + n0 + n_chunk]).astype(BF16)

    @pl.when(pl.program_id(0) % steps_per_seq == 0)
    def _():
        carry_s[...] = jnp.zeros_like(carry_s)

    lf = -_softplus(-(_dot(h, wf_ref[...]) + fb_ref[...])) * LOG2E
    tm = lf.shape[0]
    row = lax.broadcasted_iota(jnp.int32, lf.shape, 0)
    lane = lax.broadcasted_iota(jnp.int32, lf.shape, 1)
    f_cum = _chunk_cumsum(lf, row, tm) + carry_s[0:1, :]
    carry_s[...] = jnp.broadcast_to(f_cum[tm - 1:tm, :], carry_s.shape)
    fab_ref[...] = _dot(_slabs_with_one(f_cum, lane), sel_ref[...]).astype(BF16)
    _mem_attention(_dot(h, wqm_ref[...]), mqw_ref[...], mk_ref, mv_ref, mo_ref)


def _pad_lanes(v):
    return jnp.zeros((1, LANES), F32).at[0, :v.shape[0]].set(v.astype(F32))


def _pad_cols(w):
    return jnp.zeros((w.shape[0], LANES), w.dtype).at[:, :w.shape[1]].set(w)


def _in_proj_common_specs(tm, d, seq, n_mem, mw):
    assert seq % tm == 0, "a row block must not straddle two sequences"
    steps_per_seq = seq // tm
    x_spec = pl.BlockSpec((tm, d), lambda i: (i, 0))
    mem_spec = pl.BlockSpec((1, n_mem, mw), lambda i: (i // steps_per_seq, 0, 0))
    return x_spec, mem_spec


def _in_proj_dn(x2, seq, norm_w, w_in, a_log, dt_bias, memq_w, mem_k, mem_v, *, tm=1024):
    m, d = x2.shape
    tm = min(tm, seq)
    heads = a_log.shape[0]
    mix = heads * HEAD_DIM
    n_mem, mw = mem_k.shape[1], mem_k.shape[2]
    w_all = w_in.astype(BF16)
    w_ab = _pad_cols(w_in[:, 4 * mix:4 * mix + 2 * heads]).astype(BF16)
    w_qm = w_in[:, 4 * mix + 2 * heads:].astype(BF16)
    x_spec, mem_spec = _in_proj_common_specs(tm, d, seq, n_mem, mw)
    row = lambda n: pl.BlockSpec((tm, n), lambda i: (i, 0))
    return pl.pallas_call(
        functools.partial(_in_proj_dn_kernel, n_chunk=512, heads=heads, chunk=DN_CHUNK),
        grid=(m // tm,),
        in_specs=[
            x_spec, _const_spec((1, d)), _const_spec(w_all.shape), _const_spec((d, LANES)),
            _const_spec((d, mw)), _const_spec((1, LANES)), _const_spec((1, LANES)),
            _const_spec((1, HEAD_DIM)), _const_spec((3 * LANES, 2 * LANES)), mem_spec, mem_spec,
        ],
        out_specs=[row(3 * mix), row(mix), row(3 * LANES), row(2 * LANES), row(mw)],
        out_shape=[
            jax.ShapeDtypeStruct((m, 3 * mix), BF16),
            jax.ShapeDtypeStruct((m, mix), BF16),
            jax.ShapeDtypeStruct((m, 3 * LANES), BF16),
            jax.ShapeDtypeStruct((m, 2 * LANES), BF16),
            jax.ShapeDtypeStruct((m, mw), BF16),
        ],
        compiler_params=pltpu.CompilerParams(
            dimension_semantics=("arbitrary",), vmem_limit_bytes=VMEM_LIMIT),
        name="in_proj_dn",
    )(x2, norm_w.reshape(1, d), w_all, w_ab, w_qm, _pad_lanes(a_log), _pad_lanes(dt_bias),
      memq_w.reshape(1, HEAD_DIM), _diff_selector(heads), mem_k, mem_v)


def _in_proj_fox(x2, seq, norm_w, w_in, f_bias, q_norm_w, k_norm_w, memq_w, mem_k, mem_v, *, tm=1024):
    m, d = x2.shape
    tm = min(tm, seq)
    heads = f_bias.shape[0]
    mix = heads * HEAD_DIM
    n_mem, mw = mem_k.shape[1], mem_k.shape[2]
    w_all = w_in.astype(BF16)
    w_f = _pad_cols(w_in[:, 4 * mix:4 * mix + heads]).astype(BF16)
    w_qm = w_in[:, 4 * mix + heads:].astype(BF16)
    x_spec, mem_spec = _in_proj_common_specs(tm, d, seq, n_mem, mw)
    row = lambda n: pl.BlockSpec((tm, n), lambda i: (i, 0))
    return pl.pallas_call(
        functools.partial(_in_proj_fox_kernel, n_chunk=512, steps_per_seq=seq // tm),
        grid=(m // tm,),
        in_specs=[
            x_spec, _const_spec((1, d)), _const_spec(w_all.shape), _const_spec((d, LANES)),
            _const_spec((d, mw)), _const_spec((1, LANES)), _const_spec((1, HEAD_DIM)),
            _const_spec((1, HEAD_DIM)), _const_spec((1, HEAD_DIM)),
            _const_spec((3 * LANES, 2 * LANES)), mem_spec, mem_spec,
        ],
        out_specs=[row(mix), row(mix), row(mix), row(mix), row(2 * LANES), row(mw)],
        out_shape=[jax.ShapeDtypeStruct((m, mix), BF16)] * 4 + [
            jax.ShapeDtypeStruct((m, 2 * LANES), BF16),
            jax.ShapeDtypeStruct((m, mw), BF16),
        ],
        scratch_shapes=[pltpu.VMEM((8, LANES), F32)],
        compiler_params=pltpu.CompilerParams(
            dimension_semantics=("arbitrary",), vmem_limit_bytes=VMEM_LIMIT),
        name="in_proj_fox",
    )(x2, norm_w.reshape(1, d), w_all, w_f, w_qm, _pad_lanes(f_bias),
      q_norm_w.reshape(1, HEAD_DIM), k_norm_w.reshape(1, HEAD_DIM),
      memq_w.reshape(1, HEAD_DIM), _diff_selector(heads), mem_k, mem_v)


def _dn_kernel(q_ref, k_ref, v_ref, z_ref, gb3_ref, gab_ref, cwq_ref, cwk_ref, cwv_ref, onw_ref,
               o_ref, e_s, ext_s, kq_s, nc_s, op_s, eg_s, state_s, *, hb, chunk, group, heads):
    seq = q_ref.shape[0]
    n_chunks = seq // chunk
    rows = group * chunk
    hg = pl.program_id(1)

    er = lax.broadcasted_iota(jnp.int32, (3 * LANES, 2 * LANES), 0)
    ec = lax.broadcasted_iota(jnp.int32, (3 * LANES, 2 * LANES), 1)
    for j in range(hb):
        head = hg * hb + j
        want = jnp.where(ec < LANES, head, heads + head)
        e_s[j] = jnp.where(jnp.bitwise_and(er, LANES - 1) == want, 1.0, 0.0).astype(BF16)
        state_s[j] = jnp.zeros((HEAD_DIM, HEAD_DIM), F32)

    def conv_silu(ref, cw_ref, ext_ref, sl, r0, first):
        main = ref[pl.ds(r0, rows), sl].astype(F32)
        h0 = pl.multiple_of(jnp.maximum(r0 - BF16_ROWS, 0), BF16_ROWS)
        ext_ref[0:BF16_ROWS, :] = jnp.where(first, 0.0, ref[pl.ds(h0, BF16_ROWS), sl].astype(F32))
        ext_ref[BF16_ROWS:, :] = main
        cw = cw_ref[:, sl]
        acc = main * cw[CONV_WIDTH - 1:CONV_WIDTH, :]
        for s in range(1, CONV_WIDTH):
            acc = acc + ext_ref[BF16_ROWS - s:BF16_ROWS - s + rows, :] * cw[
                CONV_WIDTH - 1 - s:CONV_WIDTH - s, :]
        return acc * _sigmoid(acc)

    def l2n(x):
        return x * lax.rsqrt(jnp.sum(x * x, axis=-1, keepdims=True) + EPS)

    assert 2 * chunk == LANES
    ci = lax.broadcasted_iota(jnp.int32, (group, chunk, LANES), 1)
    cj = lax.broadcasted_iota(jnp.int32, (group, chunk, LANES), 2)
    causal = ci >= cj
    strict = ci > cj
    left = cj < chunk
    eye_right = jnp.where(cj - chunk == ci, 1.0, 0.0)
    lane2 = lax.broadcasted_iota(jnp.int32, (rows, LANES), 1)
    zeros_rows = jnp.zeros((group, chunk, LANES), BF16)
    zeros_wide = jnp.zeros((group, chunk, 2 * HEAD_DIM), BF16)
    n_double = chunk.bit_length() - 1
    shape3 = (group, chunk, HEAD_DIM)

    def bmm(a, b):
        return jnp.einsum("gij,gjk->gik", a, b, preferred_element_type=F32)

    def bmm_nt(a, b):
        return jnp.einsum("gid,gjd->gij", a, b, preferred_element_type=F32)

    def prep(gi, carry):
        r0 = pl.multiple_of(gi * rows, rows)
        c0 = pl.multiple_of(gi * group, group)
        first = gi == 0

        def front(j):
            sl = slice(j * HEAD_DIM, (j + 1) * HEAD_DIM)
            q = (l2n(conv_silu(q_ref, cwq_ref, ext_s.at[0], sl, r0, first))
                 * (HEAD_DIM ** -0.5)).reshape(shape3)
            k = l2n(conv_silu(k_ref, cwk_ref, ext_s.at[1], sl, r0, first)).reshape(shape3)
            v = conv_silu(v_ref, cwv_ref, ext_s.at[2], sl, r0, first).reshape(shape3)
            gbb = _dot(gb3_ref[pl.ds(r0, rows), :], e_s[j])
            g = gbb[:, :LANES].reshape(shape3)
            beta = gbb[:, LANES:].reshape(shape3)
            gab = gab_ref[pl.ds(r0, rows), :]
            own = jnp.right_shift(lane2, 3) == hg * hb + j
            ga = gab[:, :LANES].reshape(shape3)
            gbm = jnp.where(own, gab[:, LANES:], jnp.zeros_like(gab[:, LANES:])).reshape(shape3)
            diff = bmm_nt(ga, jnp.concatenate([gbm, zeros_rows], axis=1))
            decay = jnp.where(causal, jnp.exp(jnp.where(causal, diff, 0.0)), 0.0)
            kb = k * beta
            kq = jnp.concatenate([kb, q], axis=1).astype(BF16)
            aq = bmm_nt(kq, jnp.concatenate([k.astype(BF16), zeros_rows], axis=1))
            qk = (aq[:, chunk:] * decay)[:, :, :chunk].astype(BF16)
            y = eye_right - jnp.where(strict, aq[:, :chunk] * decay, 0.0)
            return q, k, v, g, beta, kb, qk, y

        def back(j, vals):
            q, k, v, g, beta, kb, qk, y = vals
            for _ in range(n_double):
                yb = y.astype(BF16)
                py = bmm(yb, jnp.concatenate([yb, zeros_rows], axis=1))
                y = jnp.where(left, py, y + py)
            eg = jnp.exp(g)
            rhs = jnp.concatenate([v * beta, kb * eg], axis=2).astype(BF16)
            x = bmm(y.astype(BF16), jnp.concatenate([zeros_wide, rhs], axis=1))
            xb = x.astype(BF16)
            qx = bmm(qk, xb)
            g_last = g[:, chunk - 1:chunk, :]
            kd = (k * jnp.exp(g_last - g)).astype(BF16)
            for c in range(group):
                kx = _dot_tn(kd[c], xb[c])
                nc_s[j, c0 + c] = kx[:, :HEAD_DIM]
                kq_s[j, c0 + c, :HEAD_DIM, :] = kx[:, HEAD_DIM:].astype(BF16)
            kq_s[j, pl.ds(c0, group), HEAD_DIM:, :] = (q * eg - qx[:, :, HEAD_DIM:]).astype(BF16)
            op_s[j, pl.ds(c0, group)] = qx[:, :, :HEAD_DIM]
            eg_s[j, pl.ds(c0, group)] = jnp.broadcast_to(jnp.exp(g_last), (group, 8, LANES))

        for j in range(hb):
            back(j, front(j))
        return carry

    lax.fori_loop(0, n_chunks // group, prep, 0)

    def step(c, carry):
        r0 = pl.multiple_of(c * chunk, chunk)
        for j in range(hb):
            sl = slice(j * HEAD_DIM, (j + 1) * HEAD_DIM)
            state = state_s[j]
            sb = state.astype(BF16)
            ks = _dot(kq_s[j, c], sb)
            out = ks[HEAD_DIM:] + op_s[j, c]
            state_s[j] = state * eg_s[j, c][0:1, :] + nc_s[j, c] - ks[:HEAD_DIM]
            z = z_ref[pl.ds(r0, chunk), sl].astype(F32)
            o_ref[pl.ds(r0, chunk), sl] = (
                _rms(out, onw_ref[...]) * (z * _sigmoid(z))).astype(BF16)
        return carry

    lax.fori_loop(0, n_chunks, step, 0, unroll=STEP_UNROLL)


def _deltanet(qkv, z, gb3, gab, conv_w, o_norm_w, batch, seq, heads, *, hb=4, group=16):
    m = qkv.shape[0]
    mix = heads * HEAD_DIM
    chunk = DN_CHUNK
    group = min(group, seq // chunk)
    n_chunks = seq // chunk
    hgroups = heads // hb
    wblk = hb * HEAD_DIM
    col = lambda off: pl.BlockSpec((seq, wblk), lambda b, h: (b, off + h))
    cw = lambda off: pl.BlockSpec((CONV_WIDTH, wblk), lambda b, h: (0, off + h))
    return pl.pallas_call(
        functools.partial(_dn_kernel, hb=hb, chunk=chunk, group=group, heads=heads),
        grid=(batch, hgroups),
        in_specs=[
            col(0), col(hgroups), col(2 * hgroups),
            pl.BlockSpec((seq, wblk), lambda b, h: (b, h)),
            pl.BlockSpec((seq, 3 * LANES), lambda b, h: (b, 0)),
            pl.BlockSpec((seq, 2 * LANES), lambda b, h: (b, 0)),
            cw(0), cw(hgroups), cw(2 * hgroups),
            pl.BlockSpec((1, HEAD_DIM), lambda b, h: (0, 0)),
        ],
        out_specs=pl.BlockSpec((seq, wblk), lambda b, h: (b, h)),
        out_shape=jax.ShapeDtypeStruct((m, mix), BF16),
        scratch_shapes=[
            pltpu.VMEM((hb, 3 * LANES, 2 * LANES), BF16),
            pltpu.VMEM((3, group * chunk + BF16_ROWS, HEAD_DIM), F32),
            pltpu.VMEM((hb, n_chunks, HEAD_DIM + chunk, HEAD_DIM), BF16),
            pltpu.VMEM((hb, n_chunks, HEAD_DIM, HEAD_DIM), F32),
            pltpu.VMEM((hb, n_chunks, chunk, HEAD_DIM), F32),
            pltpu.VMEM((hb, n_chunks, 8, LANES), F32),
            pltpu.VMEM((hb, HEAD_DIM, HEAD_DIM), F32),
        ],
        compiler_params=pltpu.CompilerParams(
            dimension_semantics=("arbitrary", "arbitrary"), vmem_limit_bytes=VMEM_LIMIT),
        name="deltanet",
    )(qkv, qkv, qkv, z, gb3, gab, conv_w, conv_w, conv_w, o_norm_w.reshape(1, HEAD_DIM))


def _fox_kernel(q_ref, k_ref, v_ref, gate_ref, fab_ref, o_ref, qa_s, ka_s, *, tq):
    seq = q_ref.shape[0]
    head = pl.program_id(1)
    lane = lax.broadcasted_iota(jnp.int32, (seq, LANES), 1)
    fb = fab_ref[:, LANES:]
    qa_s[:, :HEAD_DIM] = q_ref[...]
    qa_s[:, HEAD_DIM:] = fab_ref[:, :LANES]
    ka_s[:, :HEAD_DIM] = k_ref[...]
    ka_s[:, HEAD_DIM:] = jnp.where(jnp.right_shift(lane, 3) == head, fb, jnp.zeros_like(fb))

    on_or_below = (lax.broadcasted_iota(jnp.int32, (tq, tq), 1)
                   <= lax.broadcasted_iota(jnp.int32, (tq, tq), 0))

    def scores(qi):
        lo, hi = qi * tq, (qi + 1) * tq
        qa = qa_s[lo:hi, :]
        sd = jnp.where(on_or_below, _dot_nt(qa, ka_s[lo:hi, :]), -jnp.inf)
        so = _dot_nt(qa, ka_s[:lo, :]) if qi > 0 else None
        return sd, so

    n_q = seq // tq
    nxt = scores(0)
    for qi in range(n_q):
        lo, hi = qi * tq, (qi + 1) * tq
        sd, so = nxt
        if qi + 1 < n_q:
            nxt = scores(qi + 1)
        m = jnp.max(sd, axis=1, keepdims=True)
        if so is not None:
            m = jnp.maximum(m, jnp.max(so, axis=1, keepdims=True))
        pd = jnp.exp2(sd - m)
        l = jnp.sum(pd, axis=1, keepdims=True)
        o = _dot(pd.astype(BF16), v_ref[lo:hi, :])
        if so is not None:
            po = jnp.exp2(so - m)
            l = l + jnp.sum(po, axis=1, keepdims=True)
            o = o + _dot(po.astype(BF16), v_ref[:lo, :])
        gate = gate_ref[lo:hi, :].astype(F32)
        o_ref[lo:hi, :] = (o / l * _sigmoid(gate)).astype(BF16)


def _fox_attention(q, k, v, gate, fab, batch, seq, heads, *, tq=256):
    m, mix = q.shape
    tq = min(tq, seq)
    col = pl.BlockSpec((seq, HEAD_DIM), lambda b, h: (b, h))
    return pl.pallas_call(
        functools.partial(_fox_kernel, tq=tq),
        grid=(batch, heads),
        in_specs=[col, col, col, col, pl.BlockSpec((seq, 2 * LANES), lambda b, h: (b, 0))],
        out_specs=col,
        out_shape=jax.ShapeDtypeStruct((m, mix), BF16),
        scratch_shapes=[
            pltpu.VMEM((seq, 2 * HEAD_DIM), BF16),
            pltpu.VMEM((seq, 2 * HEAD_DIM), BF16),
        ],
        compiler_params=pltpu.CompilerParams(
            dimension_semantics=("arbitrary", "arbitrary"), vmem_limit_bytes=VMEM_LIMIT),
        name="fox_attention",
    )(q, k, v, gate, fab)


def _post_kernel(x_ref, mix_ref, mo_ref, wo_ref, nw_ref, w1_ref, w2_ref, y_ref, hid_s, *, f_chunk):
    mixw = mix_ref.shape[1]
    x1 = x_ref[...] + _dot(mix_ref[...], wo_ref[:mixw, :]) + _dot(mo_ref[...], wo_ref[mixw:, :])
    h = _rms(x1, nw_ref[...]).astype(BF16)
    for f0 in range(0, w1_ref.shape[1], f_chunk):
        t = jnp.maximum(_dot(h, w1_ref[:, f0:f0 + f_chunk]), 0.0)
        hid_s[:, f0:f0 + f_chunk] = (t * t).astype(BF16)
    y_ref[...] = x1 + _dot(hid_s[...], w2_ref[...])


def _post(x2, mix, mem_out, w_out, norm_w, w1, w2, *, tm=512, f_chunk=512):
    m, d = x2.shape
    mixw, mw, dff = mix.shape[1], mem_out.shape[1], w1.shape[1]
    row = lambda n: pl.BlockSpec((tm, n), lambda i: (i, 0))
    return pl.pallas_call(
        functools.partial(_post_kernel, f_chunk=f_chunk),
        grid=(m // tm,),
        in_specs=[
            row(d), row(mixw), row(mw), _const_spec((mixw + mw, d)), _const_spec((1, d)),
            _const_spec((d, dff)), _const_spec((dff, d)),
        ],
        out_specs=row(d),
        out_shape=jax.ShapeDtypeStruct((m, d), F32),
        scratch_shapes=[pltpu.VMEM((tm, dff), BF16)],
        compiler_params=pltpu.CompilerParams(
            dimension_semantics=("arbitrary",), vmem_limit_bytes=VMEM_LIMIT),
        name="post",
    )(x2, mix, mem_out, w_out.astype(BF16), norm_w.reshape(1, d), w1.astype(BF16), w2.astype(BF16))


def kernel(x, mem, mem_norm_w, w_mem_kv, mem_k_norm_w, norm1_w, dn_w_in, dn_conv_w, dn_a_log,
           dn_dt_bias, dn_o_norm_w, fox_w_in, fox_f_bias, fox_q_norm_w, fox_k_norm_w, memq_norm_w,
           w_out, norm2_w, w_mlp1, w_mlp2):
    batch, seq, d = x.shape
    depth = norm1_w.shape[0]
    mem_k, mem_v = _mem_kv(mem, mem_norm_w, w_mem_kv, mem_k_norm_w)
    x2 = x.reshape(batch * seq, d)
    for i in range(depth):
        j = i // 2
        if i % 2 == 0:
            heads = dn_a_log.shape[1]
            qkv, z, gb3, gab, mem_out = _in_proj_dn(
                x2, seq, norm1_w[i], dn_w_in[j], dn_a_log[j], dn_dt_bias[j], memq_norm_w[i],
                mem_k, mem_v)
            mix = _deltanet(qkv, z, gb3, gab, dn_conv_w[j], dn_o_norm_w[j], batch, seq, heads)
        else:
            heads = fox_f_bias.shape[1]
            q, k, v, gate, fab, mem_out = _in_proj_fox(
                x2, seq, norm1_w[i], fox_w_in[j], fox_f_bias[j], fox_q_norm_w[j],
                fox_k_norm_w[j], memq_norm_w[i], mem_k, mem_v)
            mix = _fox_attention(q, k, v, gate, fab, batch, seq, heads)
        x2 = _post(x2, mix, mem_out, w_out[i], norm2_w[i], w_mlp1[i], w_mlp2[i])
    return x2.reshape(batch, seq, d)
```

```python
import functools

import jax
import jax.numpy as jnp
from jax import lax
from jax.experimental import pallas as pl
from jax.experimental.pallas import tpu as pltpu

F32 = jnp.float32
BF16 = jnp.bfloat16

HEAD_DIM = 128
LANES = 128
BF16_ROWS = 16
MEM_HEADS = 4
CONV_WIDTH = 4
EPS = 1e-6
DN_CHUNK = 64
STEP_UNROLL = 4
VMEM_LIMIT = 56 * 1024 * 1024
LOG2E = 1.4426950408889634
ONES_LANE = LANES - 1


def _sigmoid(x):
    return 1.0 / (1.0 + jnp.exp(-x))


def _softplus(x):
    return jnp.maximum(x, 0.0) + jnp.log1p(jnp.exp(-jnp.abs(x)))


def _rms(x, w):
    return x * lax.rsqrt(jnp.mean(x * x, axis=-1, keepdims=True) + EPS) * w


def _dot(a, b):
    return jnp.dot(a, b, preferred_element_type=F32)


def _dot_nt(a, b):
    return lax.dot_general(a, b, (((1,), (1,)), ((), ())), preferred_element_type=F32)


def _dot_tn(a, b):
    return lax.dot_general(a, b, (((0,), (0,)), ((), ())), preferred_element_type=F32)


def _split3(x):
    hi = x.astype(BF16).astype(F32)
    r = x - hi
    mid = r.astype(BF16).astype(F32)
    lo = (r - mid).astype(BF16).astype(F32)
    return hi, mid, lo


DIFF_LANES = 8


def _diff_selector(heads):
    assert heads * DIFF_LANES <= LANES and heads <= ONES_LANE
    s = [[0.0] * (2 * LANES) for _ in range(3 * LANES)]
    for h in range(heads):
        for part in range(3):
            s[part * LANES + h][DIFF_LANES * h + part] = 1.0
            s[ONES_LANE][DIFF_LANES * h + 3 + part] = 1.0
            s[ONES_LANE][LANES + DIFF_LANES * h + part] = 1.0
            s[part * LANES + h][LANES + DIFF_LANES * h + 3 + part] = -1.0
    return jnp.array(s, BF16)


def _slabs_with_one(x, lane):
    hi, mid, lo = _split3(x)
    spare = lane == ONES_LANE
    return jnp.concatenate([jnp.where(spare, 1.0, hi), jnp.where(spare, 0.0, mid),
                            jnp.where(spare, 0.0, lo)], axis=1).astype(BF16)


def _chunk_cumsum(x, row_in_chunk, length):
    s = 1
    while s < length:
        x = x + jnp.where(row_in_chunk >= s, pltpu.roll(x, s, 0), 0.0)
        s *= 2
    return x


def _const_spec(shape):
    return pl.BlockSpec(shape, lambda *_: (0,) * len(shape), pipeline_mode=pl.Buffered(1))


def _mem_kv_kernel(mem_ref, nw_ref, w_ref, knw_ref, k_ref, v_ref):
    h = _rms(mem_ref[0], nw_ref[...]).astype(BF16)
    kv = _dot(h, w_ref[...])
    mw = kv.shape[1] // 2
    for hd in range(mw // HEAD_DIM):
        sl = slice(hd * HEAD_DIM, (hd + 1) * HEAD_DIM)
        k_ref[0, :, sl] = _rms(kv[:, sl], knw_ref[...]).astype(BF16)
    v_ref[0] = kv[:, mw:].astype(BF16)


def _mem_kv(mem, mem_norm_w, w_mem_kv, mem_k_norm_w):
    b, n_mem, d = mem.shape
    mw = w_mem_kv.shape[1] // 2
    return pl.pallas_call(
        _mem_kv_kernel,
        grid=(b,),
        in_specs=[
            pl.BlockSpec((1, n_mem, d), lambda i: (i, 0, 0)),
            _const_spec((1, d)),
            _const_spec((d, 2 * mw)),
            _const_spec((1, HEAD_DIM)),
        ],
        out_specs=[
            pl.BlockSpec((1, n_mem, mw), lambda i: (i, 0, 0)),
            pl.BlockSpec((1, n_mem, mw), lambda i: (i, 0, 0)),
        ],
        out_shape=[jax.ShapeDtypeStruct((b, n_mem, mw), BF16)] * 2,
        compiler_params=pltpu.CompilerParams(
            dimension_semantics=("arbitrary",), vmem_limit_bytes=VMEM_LIMIT),
        name="mem_kv",
    )(mem, mem_norm_w.reshape(1, d), w_mem_kv.astype(BF16), mem_k_norm_w.reshape(1, HEAD_DIM))


def _mem_attention(qm, mqw, mk_ref, mv_ref, mo_ref):
    scale = HEAD_DIM ** -0.5
    for hd in range(MEM_HEADS):
        sl = slice(hd * HEAD_DIM, (hd + 1) * HEAD_DIM)
        q = (_rms(qm[:, sl], mqw) * scale).astype(BF16)
        s = _dot_nt(q, mk_ref[0, :, sl])
        p = jnp.exp(s - jnp.max(s, axis=1, keepdims=True))
        o = _dot(p.astype(BF16), mv_ref[0, :, sl]) / jnp.sum(p, axis=1, keepdims=True)
        mo_ref[:, sl] = o.astype(BF16)


def _in_proj_dn_kernel(x_ref, nw_ref, wmain_ref, wab_ref, wqm_ref, alog_ref, dtb_ref, mqw_ref,
                       sel_ref, mk_ref, mv_ref, qkv_ref, z_ref, gb3_ref, gab_ref, mo_ref,
                       *, n_chunk, heads, chunk):
    h = _rms(x_ref[...], nw_ref[...]).astype(BF16)
    qkv_w = qkv_ref.shape[1]
    for n0 in range(0, qkv_w, n_chunk):
        qkv_ref[:, n0:n0 + n_chunk] = _dot(h, wmain_ref[:, n0:n0 + n_chunk]).astype(BF16)
    for n0 in range(0, z_ref.shape[1], n_chunk):
        z_ref[:, n0:n0 + n_chunk] = _dot(
            h, wmain_ref[:, qkv_w + n0:qkv_w + n0 + n_chunk]).astype(BF16)
    ab = _dot(h, wab_ref[...])
    lane = lax.broadcasted_iota(jnp.int32, ab.shape, 1)
    row = lax.broadcasted_iota(jnp.int32, ab.shape, 0)
    g = -jnp.exp(alog_ref[...]) * _softplus(ab + dtb_ref[...])
    g_cum = _chunk_cumsum(g, jnp.bitwise_and(row, chunk - 1), chunk)
    slabs = _slabs_with_one(jnp.where(lane < heads, g_cum, _sigmoid(ab)), lane)
    gb3_ref[...] = slabs
    gab_ref[...] = _dot(slabs, sel_ref[...]).astype(BF16)
    _mem_attention(_dot(h, wqm_ref[...]), mqw_ref[...], mk_ref, mv_ref, mo_ref)


def _in_proj_fox_kernel(x_ref, nw_ref, wmain_ref, wf_ref, wqm_ref, fb_ref, qnw_ref, knw_ref,
                        mqw_ref, sel_ref, mk_ref, mv_ref, q_ref, k_ref, v_ref, gate_ref, fab_ref,
                        mo_ref, carry_s, *, n_chunk, steps_per_seq):
    h = _rms(x_ref[...], nw_ref[...]).astype(BF16)
    width = q_ref.shape[1]
    scale = HEAD_DIM ** -0.5 * LOG2E
    for n0 in range(0, width, n_chunk):
        yq = _dot(h, wmain_ref[:, n0:n0 + n_chunk])
        yk = _dot(h, wmain_ref[:, width + n0:width + n0 + n_chunk])
        for c0 in range(0, n_chunk, HEAD_DIM):
            q_ref[:, n0 + c0:n0 + c0 + HEAD_DIM] = (
                _rms(yq[:, c0:c0 + HEAD_DIM], qnw_ref[...]) * scale).astype(BF16)
            k_ref[:, n0 + c0:n0 + c0 + HEAD_DIM] = _rms(
                yk[:, c0:c0 + HEAD_DIM], knw_ref[...]).astype(BF16)
        v_ref[:, n0:n0 + n_chunk] = _dot(
            h, wmain_ref[:, 2 * width + n0:2 * width + n0 + n_chunk]).astype(BF16)
        gate_ref[:, n0:n0 + n_chunk] = _dot(
            h, wmain_ref[:, 3 * width + n0:3 * width + n0 + n_chunk]).astype(BF16)

    @pl.when(pl.program_id(0) % steps_per_seq == 0)
    def _():
        carry_s[...] = jnp.zeros_like(carry_s)

    lf = -_softplus(-(_dot(h, wf_ref[...]) + fb_ref[...])) * LOG2E
    tm = lf.shape[0]
    row = lax.broadcasted_iota(jnp.int32, lf.shape, 0)
    lane = lax.broadcasted_iota(jnp.int32, lf.shape, 1)
    f_cum = _chunk_cumsum(lf, row, tm) + carry_s[0:1, :]
    carry_s[...] = jnp.broadcast_to(f_cum[tm - 1:tm, :], carry_s.shape)
    fab_ref[...] = _dot(_slabs_with_one(f_cum, lane), sel_ref[...]).astype(BF16)
    _mem_attention(_dot(h, wqm_ref[...]), mqw_ref[...], mk_ref, mv_ref, mo_ref)


def _pad_lanes(v):
    return jnp.zeros((1, LANES), F32).at[0, :v.shape[0]].set(v.astype(F32))


def _pad_cols(w):
    return jnp.zeros((w.shape[0], LANES), w.dtype).at[:, :w.shape[1]].set(w)


def _in_proj_common_specs(tm, d, seq, n_mem, mw):
    assert seq % tm == 0, "a row block must not straddle two sequences"
    steps_per_seq = seq // tm
    x_spec = pl.BlockSpec((tm, d), lambda i: (i, 0))
    mem_spec = pl.BlockSpec((1, n_mem, mw), lambda i: (i // steps_per_seq, 0, 0))
    return x_spec, mem_spec


def _in_proj_dn(x2, seq, norm_w, w_in, a_log, dt_bias, memq_w, mem_k, mem_v, *, tm=1024):
    m, d = x2.shape
    tm = min(tm, seq)
    heads = a_log.shape[0]
    mix = heads * HEAD_DIM
    n_mem, mw = mem_k.shape[1], mem_k.shape[2]
    w_all = w_in.astype(BF16)
    w_ab = _pad_cols(w_in[:, 4 * mix:4 * mix + 2 * heads]).astype(BF16)
    w_qm = w_in[:, 4 * mix + 2 * heads:].astype(BF16)
    x_spec, mem_spec = _in_proj_common_specs(tm, d, seq, n_mem, mw)
    row = lambda n: pl.BlockSpec((tm, n), lambda i: (i, 0))
    return pl.pallas_call(
        functools.partial(_in_proj_dn_kernel, n_chunk=512, heads=heads, chunk=DN_CHUNK),
        grid=(m // tm,),
        in_specs=[
            x_spec, _const_spec((1, d)), _const_spec(w_all.shape), _const_spec((d, LANES)),
            _const_spec((d, mw)), _const_spec((1, LANES)), _const_spec((1, LANES)),
            _const_spec((1, HEAD_DIM)), _const_spec((3 * LANES, 2 * LANES)), mem_spec, mem_spec,
        ],
        out_specs=[row(3 * mix), row(mix), row(3 * LANES), row(2 * LANES), row(mw)],
        out_shape=[
            jax.ShapeDtypeStruct((m, 3 * mix), BF16),
            jax.ShapeDtypeStruct((m, mix), BF16),
            jax.ShapeDtypeStruct((m, 3 * LANES), BF16),
            jax.ShapeDtypeStruct((m, 2 * LANES), BF16),
            jax.ShapeDtypeStruct((m, mw), BF16),
        ],
        compiler_params=pltpu.CompilerParams(
            dimension_semantics=("arbitrary",), vmem_limit_bytes=VMEM_LIMIT),
        name="in_proj_dn",
    )(x2, norm_w.reshape(1, d), w_all, w_ab, w_qm, _pad_lanes(a_log), _pad_lanes(dt_bias),
      memq_w.reshape(1, HEAD_DIM), _diff_selector(heads), mem_k, mem_v)


def _in_proj_fox(x2, seq, norm_w, w_in, f_bias, q_norm_w, k_norm_w, memq_w, mem_k, mem_v, *, tm=1024):
    m, d = x2.shape
    tm = min(tm, seq)
    heads = f_bias.shape[0]
    mix = heads * HEAD_DIM
    n_mem, mw = mem_k.shape[1], mem_k.shape[2]
    w_all = w_in.astype(BF16)
    w_f = _pad_cols(w_in[:, 4 * mix:4 * mix + heads]).astype(BF16)
    w_qm = w_in[:, 4 * mix + heads:].astype(BF16)
    x_spec, mem_spec = _in_proj_common_specs(tm, d, seq, n_mem, mw)
    row = lambda n: pl.BlockSpec((tm, n), lambda i: (i, 0))
    return pl.pallas_call(
        functools.partial(_in_proj_fox_kernel, n_chunk=512, steps_per_seq=seq // tm),
        grid=(m // tm,),
        in_specs=[
            x_spec, _const_spec((1, d)), _const_spec(w_all.shape), _const_spec((d, LANES)),
            _const_spec((d, mw)), _const_spec((1, LANES)), _const_spec((1, HEAD_DIM)),
            _const_spec((1, HEAD_DIM)), _const_spec((1, HEAD_DIM)),
            _const_spec((3 * LANES, 2 * LANES)), mem_spec, mem_spec,
        ],
        out_specs=[row(mix), row(mix), row(mix), row(mix), row(2 * LANES), row(mw)],
        out_shape=[jax.ShapeDtypeStruct((m, mix), BF16)] * 4 + [
            jax.ShapeDtypeStruct((m, 2 * LANES), BF16),
            jax.ShapeDtypeStruct((m, mw), BF16),
        ],
        scratch_shapes=[pltpu.VMEM((8, LANES), F32)],
        compiler_params=pltpu.CompilerParams(
            dimension_semantics=("arbitrary",), vmem_limit_bytes=VMEM_LIMIT),
        name="in_proj_fox",
    )(x2, norm_w.reshape(1, d), w_all, w_f, w_qm, _pad_lanes(f_bias),
      q_norm_w.reshape(1, HEAD_DIM), k_norm_w.reshape(1, HEAD_DIM),
      memq_w.reshape(1, HEAD_DIM), _diff_selector(heads), mem_k, mem_v)


def _dn_kernel(q_ref, k_ref, v_ref, z_ref, gb3_ref, gab_ref, cwq_ref, cwk_ref, cwv_ref, onw_ref,
               o_ref, e_s, ext_s, kq_s, nc_s, op_s, eg_s, state_s, *, hb, chunk, group, heads):
    seq = q_ref.shape[0]
    n_chunks = seq // chunk
    rows = group * chunk
    hg = pl.program_id(1)

    er = lax.broadcasted_iota(jnp.int32, (3 * LANES, 2 * LANES), 0)
    ec = lax.broadcasted_iota(jnp.int32, (3 * LANES, 2 * LANES), 1)
    for j in range(hb):
        head = hg * hb + j
        want = jnp.where(ec < LANES, head, heads + head)
        e_s[j] = jnp.where(jnp.bitwise_and(er, LANES - 1) == want, 1.0, 0.0).astype(BF16)
        state_s[j] = jnp.zeros((HEAD_DIM, HEAD_DIM), F32)

    def conv_silu(ref, cw_ref, ext_ref, sl, r0, first):
        main = ref[pl.ds(r0, rows), sl].astype(F32)
        h0 = pl.multiple_of(jnp.maximum(r0 - BF16_ROWS, 0), BF16_ROWS)
        ext_ref[0:BF16_ROWS, :] = jnp.where(first, 0.0, ref[pl.ds(h0, BF16_ROWS), sl].astype(F32))
        ext_ref[BF16_ROWS:, :] = main
        cw = cw_ref[:, sl]
        acc = main * cw[CONV_WIDTH - 1:CONV_WIDTH, :]
        for s in range(1, CONV_WIDTH):
            acc = acc + ext_ref[BF16_ROWS - s:BF16_ROWS - s + rows, :] * cw[
                CONV_WIDTH - 1 - s:CONV_WIDTH - s, :]
        return acc * _sigmoid(acc)

    def l2n(x):
        return x * lax.rsqrt(jnp.sum(x * x, axis=-1, keepdims=True) + EPS)

    assert 2 * chunk == LANES
    ci = lax.broadcasted_iota(jnp.int32, (group, chunk, LANES), 1)
    cj = lax.broadcasted_iota(jnp.int32, (group, chunk, LANES), 2)
    causal = ci >= cj
    strict = ci > cj
    eye_right = jnp.where(cj - chunk == ci, 1.0, 0.0)
    lane2 = lax.broadcasted_iota(jnp.int32, (rows, LANES), 1)
    zeros_rows = jnp.zeros((group, chunk, LANES), BF16)
    nb = hb * group
    left = lax.broadcasted_iota(jnp.int32, (nb, chunk, LANES), 2) < chunk
    zeros_rows_all = jnp.zeros((nb, chunk, LANES), BF16)
    zeros_wide_all = jnp.zeros((nb, chunk, 2 * HEAD_DIM), BF16)
    n_double = chunk.bit_length() - 1
    shape3 = (group, chunk, HEAD_DIM)

    def bmm(a, b):
        return jnp.einsum("gij,gjk->gik", a, b, preferred_element_type=F32)

    def bmm_nt(a, b):
        return jnp.einsum("gid,gjd->gij", a, b, preferred_element_type=F32)

    def prep(gi, carry):
        r0 = pl.multiple_of(gi * rows, rows)
        c0 = pl.multiple_of(gi * group, group)
        first = gi == 0

        def front(j):
            sl = slice(j * HEAD_DIM, (j + 1) * HEAD_DIM)
            q = (l2n(conv_silu(q_ref, cwq_ref, ext_s.at[0], sl, r0, first))
                 * (HEAD_DIM ** -0.5)).reshape(shape3)
            k = l2n(conv_silu(k_ref, cwk_ref, ext_s.at[1], sl, r0, first)).reshape(shape3)
            v = conv_silu(v_ref, cwv_ref, ext_s.at[2], sl, r0, first).reshape(shape3)
            gbb = _dot(gb3_ref[pl.ds(r0, rows), :], e_s[j])
            g = gbb[:, :LANES].reshape(shape3)
            beta = gbb[:, LANES:].reshape(shape3)
            gab = gab_ref[pl.ds(r0, rows), :]
            own = jnp.right_shift(lane2, 3) == hg * hb + j
            ga = gab[:, :LANES].reshape(shape3)
            gbm = jnp.where(own, gab[:, LANES:], jnp.zeros_like(gab[:, LANES:])).reshape(shape3)
            diff = bmm_nt(ga, jnp.concatenate([gbm, zeros_rows], axis=1))
            decay = jnp.where(causal, jnp.exp(jnp.where(causal, diff, 0.0)), 0.0)
            kb = k * beta
            kq = jnp.concatenate([kb, q], axis=1).astype(BF16)
            aq = bmm_nt(kq, jnp.concatenate([k.astype(BF16), zeros_rows], axis=1))
            qk = (aq[:, chunk:] * decay)[:, :, :chunk].astype(BF16)
            y = eye_right - jnp.where(strict, aq[:, :chunk] * decay, 0.0)
            return q, k, v, g, beta, kb, qk, y

        def back(per_head):
            q, k, v, g, beta, kb, qk, y = [jnp.concatenate(t, axis=0) for t in zip(*per_head)]
            for _ in range(n_double):
                yb = y.astype(BF16)
                py = bmm(yb, jnp.concatenate([yb, zeros_rows_all], axis=1))
                y = jnp.where(left, py, y + py)
            eg = jnp.exp(g)
            rhs = jnp.concatenate([v * beta, kb * eg], axis=2).astype(BF16)
            x = bmm(y.astype(BF16), jnp.concatenate([zeros_wide_all, rhs], axis=1))
            xb = x.astype(BF16)
            qx = bmm(qk, xb)
            g_last = g[:, chunk - 1:chunk, :]
            kd = (k * jnp.exp(g_last - g)).astype(BF16)
            qp = (q * eg - qx[:, :, HEAD_DIM:]).astype(BF16)
            eg_last = jnp.broadcast_to(jnp.exp(g_last), (nb, 8, LANES))
            for j in range(hb):
                for c in range(group):
                    kx = _dot_tn(kd[j * group + c], xb[j * group + c])
                    nc_s[j, c0 + c] = kx[:, :HEAD_DIM]
                    kq_s[j, c0 + c, :HEAD_DIM, :] = kx[:, HEAD_DIM:].astype(BF16)
                mine = slice(j * group, (j + 1) * group)
                kq_s[j, pl.ds(c0, group), HEAD_DIM:, :] = qp[mine]
                op_s[j, pl.ds(c0, group)] = qx[mine, :, :HEAD_DIM]
                eg_s[j, pl.ds(c0, group)] = eg_last[mine]

        back([front(j) for j in range(hb)])
        return carry

    lax.fori_loop(0, n_chunks // group, prep, 0)

    def step(c, carry):
        r0 = pl.multiple_of(c * chunk, chunk)
        for j in range(hb):
            sl = slice(j * HEAD_DIM, (j + 1) * HEAD_DIM)
            state = state_s[j]
            sb = state.astype(BF16)
            ks = _dot(kq_s[j, c], sb)
            out = ks[HEAD_DIM:] + op_s[j, c]
            state_s[j] = state * eg_s[j, c][0:1, :] + nc_s[j, c] - ks[:HEAD_DIM]
            z = z_ref[pl.ds(r0, chunk), sl].astype(F32)
            o_ref[pl.ds(r0, chunk), sl] = (
                _rms(out, onw_ref[...]) * (z * _sigmoid(z))).astype(BF16)
        return carry

    lax.fori_loop(0, n_chunks, step, 0, unroll=STEP_UNROLL)


def _deltanet(qkv, z, gb3, gab, conv_w, o_norm_w, batch, seq, heads, *, hb=4, group=8):
    m = qkv.shape[0]
    mix = heads * HEAD_DIM
    chunk = DN_CHUNK
    group = min(group, seq // chunk)
    n_chunks = seq // chunk
    hgroups = heads // hb
    wblk = hb * HEAD_DIM
    col = lambda off: pl.BlockSpec((seq, wblk), lambda b, h: (b, off + h))
    cw = lambda off: pl.BlockSpec((CONV_WIDTH, wblk), lambda b, h: (0, off + h))
    return pl.pallas_call(
        functools.partial(_dn_kernel, hb=hb, chunk=chunk, group=group, heads=heads),
        grid=(batch, hgroups),
        in_specs=[
            col(0), col(hgroups), col(2 * hgroups),
            pl.BlockSpec((seq, wblk), lambda b, h: (b, h)),
            pl.BlockSpec((seq, 3 * LANES), lambda b, h: (b, 0)),
            pl.BlockSpec((seq, 2 * LANES), lambda b, h: (b, 0)),
            cw(0), cw(hgroups), cw(2 * hgroups),
            pl.BlockSpec((1, HEAD_DIM), lambda b, h: (0, 0)),
        ],
        out_specs=pl.BlockSpec((seq, wblk), lambda b, h: (b, h)),
        out_shape=jax.ShapeDtypeStruct((m, mix), BF16),
        scratch_shapes=[
            pltpu.VMEM((hb, 3 * LANES, 2 * LANES), BF16),
            pltpu.VMEM((3, group * chunk + BF16_ROWS, HEAD_DIM), F32),
            pltpu.VMEM((hb, n_chunks, HEAD_DIM + chunk, HEAD_DIM), BF16),
            pltpu.VMEM((hb, n_chunks, HEAD_DIM, HEAD_DIM), F32),
            pltpu.VMEM((hb, n_chunks, chunk, HEAD_DIM), F32),
            pltpu.VMEM((hb, n_chunks, 8, LANES), F32),
            pltpu.VMEM((hb, HEAD_DIM, HEAD_DIM), F32),
        ],
        compiler_params=pltpu.CompilerParams(
            dimension_semantics=("arbitrary", "arbitrary"), vmem_limit_bytes=VMEM_LIMIT),
        name="deltanet",
    )(qkv, qkv, qkv, z, gb3, gab, conv_w, conv_w, conv_w, o_norm_w.reshape(1, HEAD_DIM))


def _fox_kernel(q_ref, k_ref, v_ref, gate_ref, fab_ref, o_ref, qa_s, ka_s, vt_s, *, tq):
    seq = q_ref.shape[0]
    head = pl.program_id(1)
    lane = lax.broadcasted_iota(jnp.int32, (seq, LANES), 1)
    fb = fab_ref[:, LANES:]
    qa_s[:, :HEAD_DIM] = q_ref[...]
    qa_s[:, HEAD_DIM:] = fab_ref[:, :LANES]
    ka_s[:, :HEAD_DIM] = k_ref[...]
    ka_s[:, HEAD_DIM:] = jnp.where(jnp.right_shift(lane, 3) == head, fb, jnp.zeros_like(fb))
    vt_s[...] = v_ref[...].astype(F32).T.astype(BF16)

    key_le_query = (lax.broadcasted_iota(jnp.int32, (tq, tq), 0)
                    <= lax.broadcasted_iota(jnp.int32, (tq, tq), 1))

    def score_tile(qi, kb):
        s = _dot_nt(ka_s[kb * tq:(kb + 1) * tq, :], qa_s[qi * tq:(qi + 1) * tq, :])
        return jnp.where(key_le_query, s, -jnp.inf) if kb == qi else s

    def col_max(m, s):
        sm = jnp.max(s, axis=0, keepdims=True)
        return sm if m is None else jnp.maximum(m, sm)

    n_q = seq // tq
    tiles = [score_tile(0, 0)]
    m = col_max(None, tiles[0])
    for qi in range(n_q):
        lo, hi = qi * tq, (qi + 1) * tq
        nxt_tiles, nxt_m, p_tiles, l = [], None, [], None
        n_next = qi + 2 if qi + 1 < n_q else 0
        for t in range(max(n_next, qi + 1)):
            if t < n_next:
                nxt_tiles.append(score_tile(qi + 1, t))
                nxt_m = col_max(nxt_m, nxt_tiles[-1])
            if t <= qi:
                p = jnp.exp2(tiles[t] - m)
                ps = jnp.sum(p, axis=0, keepdims=True)
                l = ps if l is None else l + ps
                p_tiles.append(p.astype(BF16))
        pt = p_tiles[0] if qi == 0 else jnp.concatenate(p_tiles, axis=0)
        ot = _dot(vt_s[:, :hi], pt)
        gate = gate_ref[lo:hi, :].astype(F32)
        o_ref[lo:hi, :] = ((ot / l).T * _sigmoid(gate)).astype(BF16)
        tiles, m = nxt_tiles, nxt_m


def _fox_attention(q, k, v, gate, fab, batch, seq, heads, *, tq=256):
    m, mix = q.shape
    tq = min(tq, seq)
    col = pl.BlockSpec((seq, HEAD_DIM), lambda b, h: (b, h))
    return pl.pallas_call(
        functools.partial(_fox_kernel, tq=tq),
        grid=(batch, heads),
        in_specs=[col, col, col, col, pl.BlockSpec((seq, 2 * LANES), lambda b, h: (b, 0))],
        out_specs=col,
        out_shape=jax.ShapeDtypeStruct((m, mix), BF16),
        scratch_shapes=[
            pltpu.VMEM((seq, 2 * HEAD_DIM), BF16),
            pltpu.VMEM((seq, 2 * HEAD_DIM), BF16),
            pltpu.VMEM((HEAD_DIM, seq), BF16),
        ],
        compiler_params=pltpu.CompilerParams(
            dimension_semantics=("arbitrary", "arbitrary"), vmem_limit_bytes=VMEM_LIMIT),
        name="fox_attention",
    )(q, k, v, gate, fab)


def _post_kernel(x_ref, mix_ref, mo_ref, wo_ref, nw_ref, w1_ref, w2_ref, y_ref, hid_s, *, f_chunk):
    mixw = mix_ref.shape[1]
    x1 = x_ref[...] + _dot(mix_ref[...], wo_ref[:mixw, :]) + _dot(mo_ref[...], wo_ref[mixw:, :])
    h = _rms(x1, nw_ref[...]).astype(BF16)
    for f0 in range(0, w1_ref.shape[1], f_chunk):
        t = jnp.maximum(_dot(h, w1_ref[:, f0:f0 + f_chunk]), 0.0)
        hid_s[:, f0:f0 + f_chunk] = (t * t).astype(BF16)
    y_ref[...] = x1 + _dot(hid_s[...], w2_ref[...])


def _post(x2, mix, mem_out, w_out, norm_w, w1, w2, layer, *, tm=512, f_chunk=512):
    m, d = x2.shape
    mixw, mw, dff = mix.shape[1], mem_out.shape[1], w1.shape[2]
    row = lambda n: pl.BlockSpec((tm, n), lambda i: (i, 0))
    layer_spec = lambda r, c: pl.BlockSpec(
        (None, r, c), lambda i: (layer, 0, 0), pipeline_mode=pl.Buffered(1))
    return pl.pallas_call(
        functools.partial(_post_kernel, f_chunk=f_chunk),
        grid=(m // tm,),
        in_specs=[
            row(d), row(mixw), row(mw), layer_spec(mixw + mw, d), _const_spec((1, d)),
            layer_spec(d, dff), layer_spec(dff, d),
        ],
        out_specs=row(d),
        out_shape=jax.ShapeDtypeStruct((m, d), F32),
        scratch_shapes=[pltpu.VMEM((tm, dff), BF16)],
        compiler_params=pltpu.CompilerParams(
            dimension_semantics=("arbitrary",), vmem_limit_bytes=VMEM_LIMIT),
        name="post",
    )(x2, mix, mem_out, w_out, norm_w.reshape(1, d), w1, w2)


def kernel(x, mem, mem_norm_w, w_mem_kv, mem_k_norm_w, norm1_w, dn_w_in, dn_conv_w, dn_a_log,
           dn_dt_bias, dn_o_norm_w, fox_w_in, fox_f_bias, fox_q_norm_w, fox_k_norm_w, memq_norm_w,
           w_out, norm2_w, w_mlp1, w_mlp2):
    batch, seq, d = x.shape
    depth = norm1_w.shape[0]
    mem_k, mem_v = _mem_kv(mem, mem_norm_w, w_mem_kv, mem_k_norm_w)
    x2 = x.reshape(batch * seq, d)
    w_out_b, w_mlp1_b, w_mlp2_b = w_out.astype(BF16), w_mlp1.astype(BF16), w_mlp2.astype(BF16)
    for i in range(depth):
        j = i // 2
        if i % 2 == 0:
            heads = dn_a_log.shape[1]
            qkv, z, gb3, gab, mem_out = _in_proj_dn(
                x2, seq, norm1_w[i], dn_w_in[j], dn_a_log[j], dn_dt_bias[j], memq_norm_w[i],
                mem_k, mem_v)
            mix = _deltanet(qkv, z, gb3, gab, dn_conv_w[j], dn_o_norm_w[j], batch, seq, heads)
        else:
            heads = fox_f_bias.shape[1]
            q, k, v, gate, fab, mem_out = _in_proj_fox(
                x2, seq, norm1_w[i], fox_w_in[j], fox_f_bias[j], fox_q_norm_w[j],
                fox_k_norm_w[j], memq_norm_w[i], mem_k, mem_v)
            mix = _fox_attention(q, k, v, gate, fab, batch, seq, heads)
        x2 = _post(x2, mix, mem_out, w_out_b, norm2_w[i], w_mlp1_b, w_mlp2_b, i)
    return x2.reshape(batch, seq, d)
```

```python
import functools

import jax
import jax.numpy as jnp
from jax import lax
from jax.experimental import pallas as pl
from jax.experimental.pallas import tpu as pltpu

F32 = jnp.float32
BF16 = jnp.bfloat16

HEAD_DIM = 128
LANES = 128
BF16_ROWS = 16
MEM_HEADS = 4
CONV_WIDTH = 4
EPS = 1e-6
DN_CHUNK = 64
STEP_UNROLL = 8
VMEM_LIMIT = 56 * 1024 * 1024
LOG2E = 1.4426950408889634
ONES_LANE = LANES - 1


def _sigmoid(x):
    return 1.0 / (1.0 + jnp.exp(-x))


def _softplus(x):
    return jnp.maximum(x, 0.0) + jnp.log1p(jnp.exp(-jnp.abs(x)))


def _rms(x, w):
    return x * lax.rsqrt(jnp.mean(x * x, axis=-1, keepdims=True) + EPS) * w


def _dot(a, b):
    return jnp.dot(a, b, preferred_element_type=F32)


def _dot_nt(a, b):
    return lax.dot_general(a, b, (((1,), (1,)), ((), ())), preferred_element_type=F32)


def _dot_tn(a, b):
    return lax.dot_general(a, b, (((0,), (0,)), ((), ())), preferred_element_type=F32)


def _split3(x):
    hi = x.astype(BF16).astype(F32)
    r = x - hi
    mid = r.astype(BF16).astype(F32)
    lo = (r - mid).astype(BF16).astype(F32)
    return hi, mid, lo


DIFF_LANES = 8


def _diff_selector(heads):
    assert heads * DIFF_LANES <= LANES and heads <= ONES_LANE
    s = [[0.0] * (2 * LANES) for _ in range(3 * LANES)]
    for h in range(heads):
        for part in range(3):
            s[part * LANES + h][DIFF_LANES * h + part] = 1.0
            s[ONES_LANE][DIFF_LANES * h + 3 + part] = 1.0
            s[ONES_LANE][LANES + DIFF_LANES * h + part] = 1.0
            s[part * LANES + h][LANES + DIFF_LANES * h + 3 + part] = -1.0
    return jnp.array(s, BF16)


def _slabs_with_one(x, lane):
    hi, mid, lo = _split3(x)
    spare = lane == ONES_LANE
    return jnp.concatenate([jnp.where(spare, 1.0, hi), jnp.where(spare, 0.0, mid),
                            jnp.where(spare, 0.0, lo)], axis=1).astype(BF16)


def _chunk_cumsum(x, row_in_chunk, length):
    s = 1
    while s < length:
        x = x + jnp.where(row_in_chunk >= s, pltpu.roll(x, s, 0), 0.0)
        s *= 2
    return x


def _const_spec(shape):
    return pl.BlockSpec(shape, lambda *_: (0,) * len(shape), pipeline_mode=pl.Buffered(1))


def _mem_kv_kernel(mem_ref, nw_ref, w_ref, knw_ref, k_ref, v_ref):
    h = _rms(mem_ref[0], nw_ref[...]).astype(BF16)
    kv = _dot(h, w_ref[...])
    mw = kv.shape[1] // 2
    for hd in range(mw // HEAD_DIM):
        sl = slice(hd * HEAD_DIM, (hd + 1) * HEAD_DIM)
        k_ref[0, :, sl] = _rms(kv[:, sl], knw_ref[...]).astype(BF16)
    v_ref[0] = kv[:, mw:].astype(BF16)


def _mem_kv(mem, mem_norm_w, w_mem_kv, mem_k_norm_w):
    b, n_mem, d = mem.shape
    mw = w_mem_kv.shape[1] // 2
    return pl.pallas_call(
        _mem_kv_kernel,
        grid=(b,),
        in_specs=[
            pl.BlockSpec((1, n_mem, d), lambda i: (i, 0, 0)),
            _const_spec((1, d)),
            _const_spec((d, 2 * mw)),
            _const_spec((1, HEAD_DIM)),
        ],
        out_specs=[
            pl.BlockSpec((1, n_mem, mw), lambda i: (i, 0, 0)),
            pl.BlockSpec((1, n_mem, mw), lambda i: (i, 0, 0)),
        ],
        out_shape=[jax.ShapeDtypeStruct((b, n_mem, mw), BF16)] * 2,
        compiler_params=pltpu.CompilerParams(
            dimension_semantics=("arbitrary",), vmem_limit_bytes=VMEM_LIMIT),
        name="mem_kv",
    )(mem, mem_norm_w.reshape(1, d), w_mem_kv.astype(BF16), mem_k_norm_w.reshape(1, HEAD_DIM))


def _mem_attention(qm, mqw, mk_ref, mv_ref, mo_ref):
    scale = HEAD_DIM ** -0.5
    for hd in range(MEM_HEADS):
        sl = slice(hd * HEAD_DIM, (hd + 1) * HEAD_DIM)
        q = (_rms(qm[:, sl], mqw) * scale).astype(BF16)
        s = _dot_nt(q, mk_ref[0, :, sl])
        p = jnp.exp(s - jnp.max(s, axis=1, keepdims=True))
        o = _dot(p.astype(BF16), mv_ref[0, :, sl]) / jnp.sum(p, axis=1, keepdims=True)
        mo_ref[:, sl] = o.astype(BF16)


def _in_proj_dn_kernel(x_ref, nw_ref, wmain_ref, wab_ref, wqm_ref, alog_ref, dtb_ref, mqw_ref,
                       sel_ref, mk_ref, mv_ref, qkv_ref, z_ref, gb3_ref, gab_ref, mo_ref,
                       *, n_chunk, heads, chunk):
    h = _rms(x_ref[...], nw_ref[...]).astype(BF16)
    _mem_attention(_dot(h, wqm_ref[...]), mqw_ref[...], mk_ref, mv_ref, mo_ref)
    ab = _dot(h, wab_ref[...])
    lane = lax.broadcasted_iota(jnp.int32, ab.shape, 1)
    row = lax.broadcasted_iota(jnp.int32, ab.shape, 0)
    g = -jnp.exp(alog_ref[...]) * _softplus(ab + dtb_ref[...])
    g_cum = _chunk_cumsum(g, jnp.bitwise_and(row, chunk - 1), chunk)
    slabs = _slabs_with_one(jnp.where(lane < heads, g_cum, _sigmoid(ab)), lane)
    gb3_ref[...] = slabs
    gab_ref[...] = _dot(slabs, sel_ref[...]).astype(BF16)
    qkv_w = qkv_ref.shape[1]
    for n0 in range(0, qkv_w, n_chunk):
        qkv_ref[:, n0:n0 + n_chunk] = _dot(h, wmain_ref[:, n0:n0 + n_chunk]).astype(BF16)
    for n0 in range(0, z_ref.shape[1], n_chunk):
        z_ref[:, n0:n0 + n_chunk] = _dot(
            h, wmain_ref[:, qkv_w + n0:qkv_w + n0 + n_chunk]).astype(BF16)


def _in_proj_fox_kernel(x_ref, nw_ref, wmain_ref, wf_ref, wqm_ref, fb_ref, qnw_ref, knw_ref,
                        mqw_ref, sel_ref, mk_ref, mv_ref, q_ref, k_ref, v_ref, gate_ref, fab_ref,
                        mo_ref, carry_s, *, n_chunk, steps_per_seq):
    h = _rms(x_ref[...], nw_ref[...]).astype(BF16)
    _mem_attention(_dot(h, wqm_ref[...]), mqw_ref[...], mk_ref, mv_ref, mo_ref)

    @pl.when(pl.program_id(0) % steps_per_seq == 0)
    def _():
        carry_s[...] = jnp.zeros_like(carry_s)

    lf = -_softplus(-(_dot(h, wf_ref[...]) + fb_ref[...])) * LOG2E
    tm = lf.shape[0]
    row = lax.broadcasted_iota(jnp.int32, lf.shape, 0)
    lane = lax.broadcasted_iota(jnp.int32, lf.shape, 1)
    f_cum = _chunk_cumsum(lf, row, tm) + carry_s[0:1, :]
    carry_s[...] = jnp.broadcast_to(f_cum[tm - 1:tm, :], carry_s.shape)
    fab_ref[...] = _dot(_slabs_with_one(f_cum, lane), sel_ref[...]).astype(BF16)

    width = q_ref.shape[1]
    scale = HEAD_DIM ** -0.5 * LOG2E
    for n0 in range(0, width, n_chunk):
        yq = _dot(h, wmain_ref[:, n0:n0 + n_chunk])
        yk = _dot(h, wmain_ref[:, width + n0:width + n0 + n_chunk])
        for c0 in range(0, n_chunk, HEAD_DIM):
            q_ref[:, n0 + c0:n0 + c0 + HEAD_DIM] = (
                _rms(yq[:, c0:c0 + HEAD_DIM], qnw_ref[...]) * scale).astype(BF16)
            k_ref[:, n0 + c0:n0 + c0 + HEAD_DIM] = _rms(
                yk[:, c0:c0 + HEAD_DIM], knw_ref[...]).astype(BF16)
        v_ref[:, n0:n0 + n_chunk] = _dot(
            h, wmain_ref[:, 2 * width + n0:2 * width + n0 + n_chunk]).astype(BF16)
        gate_ref[:, n0:n0 + n_chunk] = _dot(
            h, wmain_ref[:, 3 * width + n0:3 * width + n0 + n_chunk]).astype(BF16)


def _pad_lanes(v):
    return jnp.zeros((1, LANES), F32).at[0, :v.shape[0]].set(v.astype(F32))


def _pad_cols(w):
    return jnp.zeros((w.shape[0], LANES), w.dtype).at[:, :w.shape[1]].set(w)


def _in_proj_common_specs(tm, d, seq, n_mem, mw):
    assert seq % tm == 0, "a row block must not straddle two sequences"
    steps_per_seq = seq // tm
    x_spec = pl.BlockSpec((tm, d), lambda i: (i, 0))
    mem_spec = pl.BlockSpec((1, n_mem, mw), lambda i: (i // steps_per_seq, 0, 0))
    return x_spec, mem_spec


def _in_proj_dn(x2, seq, norm_w, w_in, a_log, dt_bias, memq_w, mem_k, mem_v, *, tm=1024):
    m, d = x2.shape
    tm = min(tm, seq)
    heads = a_log.shape[0]
    mix = heads * HEAD_DIM
    n_mem, mw = mem_k.shape[1], mem_k.shape[2]
    w_all = w_in.astype(BF16)
    w_ab = _pad_cols(w_in[:, 4 * mix:4 * mix + 2 * heads]).astype(BF16)
    w_qm = w_in[:, 4 * mix + 2 * heads:].astype(BF16)
    x_spec, mem_spec = _in_proj_common_specs(tm, d, seq, n_mem, mw)
    row = lambda n: pl.BlockSpec((tm, n), lambda i: (i, 0))
    return pl.pallas_call(
        functools.partial(_in_proj_dn_kernel, n_chunk=512, heads=heads, chunk=DN_CHUNK),
        grid=(m // tm,),
        in_specs=[
            x_spec, _const_spec((1, d)), _const_spec(w_all.shape), _const_spec((d, LANES)),
            _const_spec((d, mw)), _const_spec((1, LANES)), _const_spec((1, LANES)),
            _const_spec((1, HEAD_DIM)), _const_spec((3 * LANES, 2 * LANES)), mem_spec, mem_spec,
        ],
        out_specs=[row(3 * mix), row(mix), row(3 * LANES), row(2 * LANES), row(mw)],
        out_shape=[
            jax.ShapeDtypeStruct((m, 3 * mix), BF16),
            jax.ShapeDtypeStruct((m, mix), BF16),
            jax.ShapeDtypeStruct((m, 3 * LANES), BF16),
            jax.ShapeDtypeStruct((m, 2 * LANES), BF16),
            jax.ShapeDtypeStruct((m, mw), BF16),
        ],
        compiler_params=pltpu.CompilerParams(
            dimension_semantics=("arbitrary",), vmem_limit_bytes=VMEM_LIMIT),
        name="in_proj_dn",
    )(x2, norm_w.reshape(1, d), w_all, w_ab, w_qm, _pad_lanes(a_log), _pad_lanes(dt_bias),
      memq_w.reshape(1, HEAD_DIM), _diff_selector(heads), mem_k, mem_v)


def _in_proj_fox(x2, seq, norm_w, w_in, f_bias, q_norm_w, k_norm_w, memq_w, mem_k, mem_v, *, tm=1024):
    m, d = x2.shape
    tm = min(tm, seq)
    heads = f_bias.shape[0]
    mix = heads * HEAD_DIM
    n_mem, mw = mem_k.shape[1], mem_k.shape[2]
    w_all = w_in.astype(BF16)
    w_f = _pad_cols(w_in[:, 4 * mix:4 * mix + heads]).astype(BF16)
    w_qm = w_in[:, 4 * mix + heads:].astype(BF16)
    x_spec, mem_spec = _in_proj_common_specs(tm, d, seq, n_mem, mw)
    row = lambda n: pl.BlockSpec((tm, n), lambda i: (i, 0))
    return pl.pallas_call(
        functools.partial(_in_proj_fox_kernel, n_chunk=512, steps_per_seq=seq // tm),
        grid=(m // tm,),
        in_specs=[
            x_spec, _const_spec((1, d)), _const_spec(w_all.shape), _const_spec((d, LANES)),
            _const_spec((d, mw)), _const_spec((1, LANES)), _const_spec((1, HEAD_DIM)),
            _const_spec((1, HEAD_DIM)), _const_spec((1, HEAD_DIM)),
            _const_spec((3 * LANES, 2 * LANES)), mem_spec, mem_spec,
        ],
        out_specs=[row(mix), row(mix), row(mix), row(mix), row(2 * LANES), row(mw)],
        out_shape=[jax.ShapeDtypeStruct((m, mix), BF16)] * 4 + [
            jax.ShapeDtypeStruct((m, 2 * LANES), BF16),
            jax.ShapeDtypeStruct((m, mw), BF16),
        ],
        scratch_shapes=[pltpu.VMEM((8, LANES), F32)],
        compiler_params=pltpu.CompilerParams(
            dimension_semantics=("arbitrary",), vmem_limit_bytes=VMEM_LIMIT),
        name="in_proj_fox",
    )(x2, norm_w.reshape(1, d), w_all, w_f, w_qm, _pad_lanes(f_bias),
      q_norm_w.reshape(1, HEAD_DIM), k_norm_w.reshape(1, HEAD_DIM),
      memq_w.reshape(1, HEAD_DIM), _diff_selector(heads), mem_k, mem_v)


def _dn_kernel(q_ref, k_ref, v_ref, z_ref, gb3_ref, gab_ref, cwq_ref, cwk_ref, cwv_ref, onw_ref,
               o_ref, e_s, ext_s, kq_s, nc_s, op_s, eg_s, state_s, *, hb, chunk, group, heads):
    seq = q_ref.shape[0]
    n_chunks = seq // chunk
    rows = group * chunk
    hg = pl.program_id(1)

    er = lax.broadcasted_iota(jnp.int32, (3 * LANES, 2 * LANES), 0)
    ec = lax.broadcasted_iota(jnp.int32, (3 * LANES, 2 * LANES), 1)
    for j in range(hb):
        head = hg * hb + j
        want = jnp.where(ec < LANES, head, heads + head)
        e_s[j] = jnp.where(jnp.bitwise_and(er, LANES - 1) == want, 1.0, 0.0).astype(BF16)
        state_s[j] = jnp.zeros((HEAD_DIM, HEAD_DIM), F32)

    def conv_silu(ref, cw_ref, ext_ref, sl, r0, first):
        main = ref[pl.ds(r0, rows), sl].astype(F32)
        h0 = pl.multiple_of(jnp.maximum(r0 - BF16_ROWS, 0), BF16_ROWS)
        ext_ref[0:BF16_ROWS, :] = jnp.where(first, 0.0, ref[pl.ds(h0, BF16_ROWS), sl].astype(F32))
        ext_ref[BF16_ROWS:, :] = main
        cw = cw_ref[:, sl]
        acc = main * cw[CONV_WIDTH - 1:CONV_WIDTH, :]
        for s in range(1, CONV_WIDTH):
            acc = acc + ext_ref[BF16_ROWS - s:BF16_ROWS - s + rows, :] * cw[
                CONV_WIDTH - 1 - s:CONV_WIDTH - s, :]
        return acc * _sigmoid(acc)

    def l2n(x):
        return x * lax.rsqrt(jnp.sum(x * x, axis=-1, keepdims=True) + EPS)

    assert 2 * chunk == LANES
    ci = lax.broadcasted_iota(jnp.int32, (group, chunk, LANES), 1)
    cj = lax.broadcasted_iota(jnp.int32, (group, chunk, LANES), 2)
    causal = ci >= cj
    strict = ci > cj
    eye_right = jnp.where(cj - chunk == ci, 1.0, 0.0)
    lane2 = lax.broadcasted_iota(jnp.int32, (rows, LANES), 1)
    zeros_rows = jnp.zeros((group, chunk, LANES), BF16)
    nb = hb * group
    left = lax.broadcasted_iota(jnp.int32, (nb, chunk, LANES), 2) < chunk
    zeros_rows_all = jnp.zeros((nb, chunk, LANES), BF16)
    zeros_wide_all = jnp.zeros((nb, chunk, 2 * HEAD_DIM), BF16)
    n_double = chunk.bit_length() - 1
    shape3 = (group, chunk, HEAD_DIM)

    def bmm(a, b):
        return jnp.einsum("gij,gjk->gik", a, b, preferred_element_type=F32)

    def bmm_nt(a, b):
        return jnp.einsum("gid,gjd->gij", a, b, preferred_element_type=F32)

    def prep(gi, carry):
        r0 = pl.multiple_of(gi * rows, rows)
        c0 = pl.multiple_of(gi * group, group)
        first = gi == 0

        def front(j):
            sl = slice(j * HEAD_DIM, (j + 1) * HEAD_DIM)
            q = (l2n(conv_silu(q_ref, cwq_ref, ext_s.at[0], sl, r0, first))
                 * (HEAD_DIM ** -0.5)).reshape(shape3)
            k = l2n(conv_silu(k_ref, cwk_ref, ext_s.at[1], sl, r0, first)).reshape(shape3)
            v = conv_silu(v_ref, cwv_ref, ext_s.at[2], sl, r0, first).reshape(shape3)
            gbb = _dot(gb3_ref[pl.ds(r0, rows), :], e_s[j])
            g = gbb[:, :LANES].reshape(shape3)
            beta = gbb[:, LANES:].reshape(shape3)
            gab = gab_ref[pl.ds(r0, rows), :]
            own = jnp.right_shift(lane2, 3) == hg * hb + j
            ga = gab[:, :LANES].reshape(shape3)
            gbm = jnp.where(own, gab[:, LANES:], jnp.zeros_like(gab[:, LANES:])).reshape(shape3)
            diff = bmm_nt(ga, jnp.concatenate([gbm, zeros_rows], axis=1))
            decay = jnp.where(causal, jnp.exp(jnp.where(causal, diff, 0.0)), 0.0)
            kb = k * beta
            kq = jnp.concatenate([kb, q], axis=1).astype(BF16)
            aq = bmm_nt(kq, jnp.concatenate([k.astype(BF16), zeros_rows], axis=1))
            qk = (aq[:, chunk:] * decay)[:, :, :chunk].astype(BF16)
            y = eye_right - jnp.where(strict, aq[:, :chunk] * decay, 0.0)
            return q, k, v, g, beta, kb, qk, y

        def back(per_head):
            q, k, v, g, beta, kb, qk, y = [jnp.concatenate(t, axis=0) for t in zip(*per_head)]
            for _ in range(n_double):
                yb = y.astype(BF16)
                py = bmm(yb, jnp.concatenate([yb, zeros_rows_all], axis=1))
                y = jnp.where(left, py, y + py)
            eg = jnp.exp(g)
            rhs = jnp.concatenate([v * beta, kb * eg], axis=2).astype(BF16)
            x = bmm(y.astype(BF16), jnp.concatenate([zeros_wide_all, rhs], axis=1))
            xb = x.astype(BF16)
            qx = bmm(qk, xb)
            g_last = g[:, chunk - 1:chunk, :]
            kd = (k * jnp.exp(g_last - g)).astype(BF16)
            qp = (q * eg - qx[:, :, HEAD_DIM:]).astype(BF16)
            eg_last = jnp.broadcast_to(jnp.exp(g_last), (nb, 8, LANES))
            for j in range(hb):
                for c in range(group):
                    kx = _dot_tn(kd[j * group + c], xb[j * group + c])
                    nc_s[j, c0 + c] = kx[:, :HEAD_DIM]
                    kq_s[j, c0 + c, :HEAD_DIM, :] = kx[:, HEAD_DIM:].astype(BF16)
                mine = slice(j * group, (j + 1) * group)
                kq_s[j, pl.ds(c0, group), HEAD_DIM:, :] = qp[mine]
                op_s[j, pl.ds(c0, group)] = qx[mine, :, :HEAD_DIM]
                eg_s[j, pl.ds(c0, group)] = eg_last[mine]

        back([front(j) for j in range(hb)])
        return carry

    lax.fori_loop(0, n_chunks // group, prep, 0)

    def step(c, carry):
        r0 = pl.multiple_of(c * chunk, chunk)
        for j in range(hb):
            sl = slice(j * HEAD_DIM, (j + 1) * HEAD_DIM)
            state = state_s[j]
            sb = state.astype(BF16)
            ks = _dot(kq_s[j, c], sb)
            out = ks[HEAD_DIM:] + op_s[j, c]
            state_s[j] = state * eg_s[j, c][0:1, :] + nc_s[j, c] - ks[:HEAD_DIM]
            z = z_ref[pl.ds(r0, chunk), sl].astype(F32)
            o_ref[pl.ds(r0, chunk), sl] = (
                _rms(out, onw_ref[...]) * (z * _sigmoid(z))).astype(BF16)
        return carry

    lax.fori_loop(0, n_chunks, step, 0, unroll=STEP_UNROLL)


def _deltanet(qkv, z, gb3, gab, conv_w, o_norm_w, batch, seq, heads, *, hb=4, group=8):
    m = qkv.shape[0]
    mix = heads * HEAD_DIM
    chunk = DN_CHUNK
    group = min(group, seq // chunk)
    n_chunks = seq // chunk
    hgroups = heads // hb
    wblk = hb * HEAD_DIM
    col = lambda off: pl.BlockSpec((seq, wblk), lambda b, h: (b, off + h))
    cw = lambda off: pl.BlockSpec((CONV_WIDTH, wblk), lambda b, h: (0, off + h))
    return pl.pallas_call(
        functools.partial(_dn_kernel, hb=hb, chunk=chunk, group=group, heads=heads),
        grid=(batch, hgroups),
        in_specs=[
            col(0), col(hgroups), col(2 * hgroups),
            pl.BlockSpec((seq, wblk), lambda b, h: (b, h)),
            pl.BlockSpec((seq, 3 * LANES), lambda b, h: (b, 0)),
            pl.BlockSpec((seq, 2 * LANES), lambda b, h: (b, 0)),
            cw(0), cw(hgroups), cw(2 * hgroups),
            pl.BlockSpec((1, HEAD_DIM), lambda b, h: (0, 0)),
        ],
        out_specs=pl.BlockSpec((seq, wblk), lambda b, h: (b, h)),
        out_shape=jax.ShapeDtypeStruct((m, mix), BF16),
        scratch_shapes=[
            pltpu.VMEM((hb, 3 * LANES, 2 * LANES), BF16),
            pltpu.VMEM((3, group * chunk + BF16_ROWS, HEAD_DIM), F32),
            pltpu.VMEM((hb, n_chunks, HEAD_DIM + chunk, HEAD_DIM), BF16),
            pltpu.VMEM((hb, n_chunks, HEAD_DIM, HEAD_DIM), F32),
            pltpu.VMEM((hb, n_chunks, chunk, HEAD_DIM), F32),
            pltpu.VMEM((hb, n_chunks, 8, LANES), F32),
            pltpu.VMEM((hb, HEAD_DIM, HEAD_DIM), F32),
        ],
        compiler_params=pltpu.CompilerParams(
            dimension_semantics=("arbitrary", "arbitrary"), vmem_limit_bytes=VMEM_LIMIT),
        name="deltanet",
    )(qkv, qkv, qkv, z, gb3, gab, conv_w, conv_w, conv_w, o_norm_w.reshape(1, HEAD_DIM))


def _fox_kernel(q_ref, k_ref, v_ref, gate_ref, fab_ref, o_ref, qa_s, ka_s, vt_s, *, tq, hb):
    seq = q_ref.shape[0]
    n_q = seq // tq
    lane = lax.broadcasted_iota(jnp.int32, (seq, LANES), 1)
    key_le_query = (lax.broadcasted_iota(jnp.int32, (tq, tq), 0)
                    <= lax.broadcasted_iota(jnp.int32, (tq, tq), 1))

    def col_max(m, s):
        sm = jnp.max(s, axis=0, keepdims=True)
        return sm if m is None else jnp.maximum(m, sm)

    def head_program(j):
        sl = slice(j * HEAD_DIM, (j + 1) * HEAD_DIM)
        head = pl.program_id(1) * hb + j
        fb = fab_ref[:, LANES:]
        qa_s[j, :, :HEAD_DIM] = q_ref[:, sl]
        qa_s[j, :, HEAD_DIM:] = fab_ref[:, :LANES]
        ka_s[j, :, :HEAD_DIM] = k_ref[:, sl]
        ka_s[j, :, HEAD_DIM:] = jnp.where(jnp.right_shift(lane, 3) == head, fb, jnp.zeros_like(fb))
        vt_s[j] = v_ref[:, sl].astype(F32).T.astype(BF16)

        def score_tile(qi, kb):
            s = _dot_nt(ka_s[j, kb * tq:(kb + 1) * tq, :], qa_s[j, qi * tq:(qi + 1) * tq, :])
            return jnp.where(key_le_query, s, -jnp.inf) if kb == qi else s

        tiles = [score_tile(0, 0)]
        m = col_max(None, tiles[0])
        yield
        for qi in range(n_q):
            lo, hi = qi * tq, (qi + 1) * tq
            nxt_tiles, nxt_m, p_tiles, l = [], None, [], None
            n_next = qi + 2 if qi + 1 < n_q else 0
            for t in range(max(n_next, qi + 1)):
                if t < n_next:
                    nxt_tiles.append(score_tile(qi + 1, t))
                    nxt_m = col_max(nxt_m, nxt_tiles[-1])
                if t <= qi:
                    p = jnp.exp2(tiles[t] - m)
                    ps = jnp.sum(p, axis=0, keepdims=True)
                    l = ps if l is None else l + ps
                    p_tiles.append(p.astype(BF16))
            pt = p_tiles[0] if qi == 0 else jnp.concatenate(p_tiles, axis=0)
            ot = _dot(vt_s[j, :, :hi], pt)
            gate = gate_ref[lo:hi, sl].astype(F32)
            o_ref[lo:hi, sl] = ((ot / l).T * _sigmoid(gate)).astype(BF16)
            tiles, m = nxt_tiles, nxt_m
            yield

    programs = [head_program(j) for j in range(hb)]
    for _ in range(n_q + 1):
        for prog in programs:
            next(prog)


def _fox_attention(q, k, v, gate, fab, batch, seq, heads, *, tq=256, hb=4):
    m, mix = q.shape
    tq = min(tq, seq)
    col = pl.BlockSpec((seq, hb * HEAD_DIM), lambda b, h: (b, h))
    return pl.pallas_call(
        functools.partial(_fox_kernel, tq=tq, hb=hb),
        grid=(batch, heads // hb),
        in_specs=[col, col, col, col, pl.BlockSpec((seq, 2 * LANES), lambda b, h: (b, 0))],
        out_specs=col,
        out_shape=jax.ShapeDtypeStruct((m, mix), BF16),
        scratch_shapes=[
            pltpu.VMEM((hb, seq, 2 * HEAD_DIM), BF16),
            pltpu.VMEM((hb, seq, 2 * HEAD_DIM), BF16),
            pltpu.VMEM((hb, HEAD_DIM, seq), BF16),
        ],
        compiler_params=pltpu.CompilerParams(
            dimension_semantics=("arbitrary", "arbitrary"), vmem_limit_bytes=VMEM_LIMIT),
        name="fox_attention",
    )(q, k, v, gate, fab)


def _post_kernel(x_ref, mix_ref, mo_ref, wo_ref, nw_ref, w1_ref, w2_ref, y_ref, hid_s, *, f_chunk):
    mixw = mix_ref.shape[1]
    x1 = x_ref[...] + _dot(mix_ref[...], wo_ref[:mixw, :]) + _dot(mo_ref[...], wo_ref[mixw:, :])
    h = _rms(x1, nw_ref[...]).astype(BF16)
    for f0 in range(0, w1_ref.shape[1], f_chunk):
        t = jnp.maximum(_dot(h, w1_ref[:, f0:f0 + f_chunk]), 0.0)
        hid_s[:, f0:f0 + f_chunk] = (t * t).astype(BF16)
    y_ref[...] = x1 + _dot(hid_s[...], w2_ref[...])


def _post(x2, mix, mem_out, w_out, norm_w, w1, w2, layer, *, tm=512, f_chunk=512):
    m, d = x2.shape
    mixw, mw, dff = mix.shape[1], mem_out.shape[1], w1.shape[2]
    row = lambda n: pl.BlockSpec((tm, n), lambda i: (i, 0))
    layer_spec = lambda r, c: pl.BlockSpec(
        (None, r, c), lambda i: (layer, 0, 0), pipeline_mode=pl.Buffered(1))
    return pl.pallas_call(
        functools.partial(_post_kernel, f_chunk=f_chunk),
        grid=(m // tm,),
        in_specs=[
            row(d), row(mixw), row(mw), layer_spec(mixw + mw, d), _const_spec((1, d)),
            layer_spec(d, dff), layer_spec(dff, d),
        ],
        out_specs=row(d),
        out_shape=jax.ShapeDtypeStruct((m, d), F32),
        scratch_shapes=[pltpu.VMEM((tm, dff), BF16)],
        compiler_params=pltpu.CompilerParams(
            dimension_semantics=("arbitrary",), vmem_limit_bytes=VMEM_LIMIT),
        name="post",
    )(x2, mix, mem_out, w_out, norm_w.reshape(1, d), w1, w2)


def kernel(x, mem, mem_norm_w, w_mem_kv, mem_k_norm_w, norm1_w, dn_w_in, dn_conv_w, dn_a_log,
           dn_dt_bias, dn_o_norm_w, fox_w_in, fox_f_bias, fox_q_norm_w, fox_k_norm_w, memq_norm_w,
           w_out, norm2_w, w_mlp1, w_mlp2):
    batch, seq, d = x.shape
    depth = norm1_w.shape[0]
    mem_k, mem_v = _mem_kv(mem, mem_norm_w, w_mem_kv, mem_k_norm_w)
    x2 = x.reshape(batch * seq, d)
    w_out_b, w_mlp1_b, w_mlp2_b = w_out.astype(BF16), w_mlp1.astype(BF16), w_mlp2.astype(BF16)
    for i in range(depth):
        j = i // 2
        if i % 2 == 0:
            heads = dn_a_log.shape[1]
            qkv, z, gb3, gab, mem_out = _in_proj_dn(
                x2, seq, norm1_w[i], dn_w_in[j], dn_a_log[j], dn_dt_bias[j], memq_norm_w[i],
                mem_k, mem_v)
            mix = _deltanet(qkv, z, gb3, gab, dn_conv_w[j], dn_o_norm_w[j], batch, seq, heads)
        else:
            heads = fox_f_bias.shape[1]
            q, k, v, gate, fab, mem_out = _in_proj_fox(
                x2, seq, norm1_w[i], fox_w_in[j], fox_f_bias[j], fox_q_norm_w[j],
                fox_k_norm_w[j], memq_norm_w[i], mem_k, mem_v)
            mix = _fox_attention(q, k, v, gate, fab, batch, seq, heads)
        x2 = _post(x2, mix, mem_out, w_out_b, norm2_w[i], w_mlp1_b, w_mlp2_b, i)
    return x2.reshape(batch, seq, d)
```

```python
import functools

import jax
import jax.numpy as jnp
from jax import lax
from jax.experimental import pallas as pl
from jax.experimental.pallas import tpu as pltpu

F32 = jnp.float32
BF16 = jnp.bfloat16

HEAD_DIM = 128
LANES = 128
BF16_ROWS = 16
MEM_HEADS = 4
CONV_WIDTH = 4
EPS = 1e-6
DN_CHUNK = 64
STEP_UNROLL = 8
VMEM_LIMIT = 56 * 1024 * 1024
LOG2E = 1.4426950408889634
ONES_LANE = LANES - 1


def _sigmoid(x):
    return 1.0 / (1.0 + jnp.exp(-x))


def _softplus(x):
    return jnp.maximum(x, 0.0) + jnp.log1p(jnp.exp(-jnp.abs(x)))


def _rms(x, w):
    return x * lax.rsqrt(jnp.mean(x * x, axis=-1, keepdims=True) + EPS) * w


def _dot(a, b):
    return jnp.dot(a, b, preferred_element_type=F32)


def _dot_nt(a, b):
    return lax.dot_general(a, b, (((1,), (1,)), ((), ())), preferred_element_type=F32)


def _dot_tn(a, b):
    return lax.dot_general(a, b, (((0,), (0,)), ((), ())), preferred_element_type=F32)


def _split3(x):
    hi = x.astype(BF16).astype(F32)
    r = x - hi
    mid = r.astype(BF16).astype(F32)
    lo = (r - mid).astype(BF16).astype(F32)
    return hi, mid, lo


DIFF_LANES = 8


def _diff_selector(heads):
    assert heads * DIFF_LANES <= LANES and heads <= ONES_LANE
    s = [[0.0] * (2 * LANES) for _ in range(3 * LANES)]
    for h in range(heads):
        for part in range(3):
            s[part * LANES + h][DIFF_LANES * h + part] = 1.0
            s[ONES_LANE][DIFF_LANES * h + 3 + part] = 1.0
            s[ONES_LANE][LANES + DIFF_LANES * h + part] = 1.0
            s[part * LANES + h][LANES + DIFF_LANES * h + 3 + part] = -1.0
    return jnp.array(s, BF16)


def _slabs_with_one(x, lane):
    hi, mid, lo = _split3(x)
    spare = lane == ONES_LANE
    return jnp.concatenate([jnp.where(spare, 1.0, hi), jnp.where(spare, 0.0, mid),
                            jnp.where(spare, 0.0, lo)], axis=1).astype(BF16)


def _chunk_cumsum(x, row_in_chunk, length):
    s = 1
    while s < length:
        x = x + jnp.where(row_in_chunk >= s, pltpu.roll(x, s, 0), 0.0)
        s *= 2
    return x


def _const_spec(shape):
    return pl.BlockSpec(shape, lambda *_: (0,) * len(shape), pipeline_mode=pl.Buffered(1))


def _mem_kv_kernel(mem_ref, nw_ref, w_ref, knw_ref, k_ref, v_ref):
    h = _rms(mem_ref[0], nw_ref[...]).astype(BF16)
    kv = _dot(h, w_ref[...])
    mw = kv.shape[1] // 2
    for hd in range(mw // HEAD_DIM):
        sl = slice(hd * HEAD_DIM, (hd + 1) * HEAD_DIM)
        k_ref[0, :, sl] = _rms(kv[:, sl], knw_ref[...]).astype(BF16)
    v_ref[0] = kv[:, mw:].astype(BF16)


def _mem_kv(mem, mem_norm_w, w_mem_kv, mem_k_norm_w):
    b, n_mem, d = mem.shape
    mw = w_mem_kv.shape[1] // 2
    return pl.pallas_call(
        _mem_kv_kernel,
        grid=(b,),
        in_specs=[
            pl.BlockSpec((1, n_mem, d), lambda i: (i, 0, 0)),
            _const_spec((1, d)),
            _const_spec((d, 2 * mw)),
            _const_spec((1, HEAD_DIM)),
        ],
        out_specs=[
            pl.BlockSpec((1, n_mem, mw), lambda i: (i, 0, 0)),
            pl.BlockSpec((1, n_mem, mw), lambda i: (i, 0, 0)),
        ],
        out_shape=[jax.ShapeDtypeStruct((b, n_mem, mw), BF16)] * 2,
        compiler_params=pltpu.CompilerParams(
            dimension_semantics=("arbitrary",), vmem_limit_bytes=VMEM_LIMIT),
        name="mem_kv",
    )(mem, mem_norm_w.reshape(1, d), w_mem_kv.astype(BF16), mem_k_norm_w.reshape(1, HEAD_DIM))


def _mem_attention(qm, mqw, mk_ref, mv_ref, mo_ref):
    scale = HEAD_DIM ** -0.5
    for hd in range(MEM_HEADS):
        sl = slice(hd * HEAD_DIM, (hd + 1) * HEAD_DIM)
        q = (_rms(qm[:, sl], mqw) * scale).astype(BF16)
        s = _dot_nt(q, mk_ref[0, :, sl])
        p = jnp.exp(s - jnp.max(s, axis=1, keepdims=True))
        o = _dot(p.astype(BF16), mv_ref[0, :, sl]) / jnp.sum(p, axis=1, keepdims=True)
        mo_ref[:, sl] = o.astype(BF16)


def _in_proj_dn_kernel(x_ref, nw_ref, wmain_ref, wab_ref, wqm_ref, alog_ref, dtb_ref, mqw_ref,
                       sel_ref, mk_ref, mv_ref, qkv_ref, z_ref, gb3_ref, gab_ref, mo_ref,
                       *, n_chunk, heads, chunk):
    h = _rms(x_ref[...], nw_ref[...]).astype(BF16)
    _mem_attention(_dot(h, wqm_ref[...]), mqw_ref[...], mk_ref, mv_ref, mo_ref)
    ab = _dot(h, wab_ref[...])
    lane = lax.broadcasted_iota(jnp.int32, ab.shape, 1)
    row = lax.broadcasted_iota(jnp.int32, ab.shape, 0)
    g = -jnp.exp(alog_ref[...]) * _softplus(ab + dtb_ref[...])
    g_cum = _chunk_cumsum(g, jnp.bitwise_and(row, chunk - 1), chunk)
    slabs = _slabs_with_one(jnp.where(lane < heads, g_cum, _sigmoid(ab)), lane)
    gb3_ref[...] = slabs
    gab_ref[...] = _dot(slabs, sel_ref[...]).astype(BF16)
    qkv_w = qkv_ref.shape[1]
    for n0 in range(0, qkv_w, n_chunk):
        qkv_ref[:, n0:n0 + n_chunk] = _dot(h, wmain_ref[:, n0:n0 + n_chunk]).astype(BF16)
    for n0 in range(0, z_ref.shape[1], n_chunk):
        z_ref[:, n0:n0 + n_chunk] = _dot(
            h, wmain_ref[:, qkv_w + n0:qkv_w + n0 + n_chunk]).astype(BF16)


def _in_proj_fox_kernel(x_ref, nw_ref, wmain_ref, wf_ref, wqm_ref, fb_ref, qnw_ref, knw_ref,
                        mqw_ref, sel_ref, mk_ref, mv_ref, q_ref, k_ref, v_ref, gate_ref, fab_ref,
                        mo_ref, carry_s, *, n_chunk, steps_per_seq):
    h = _rms(x_ref[...], nw_ref[...]).astype(BF16)
    _mem_attention(_dot(h, wqm_ref[...]), mqw_ref[...], mk_ref, mv_ref, mo_ref)

    @pl.when(pl.program_id(0) % steps_per_seq == 0)
    def _():
        carry_s[...] = jnp.zeros_like(carry_s)

    lf = -_softplus(-(_dot(h, wf_ref[...]) + fb_ref[...])) * LOG2E
    tm = lf.shape[0]
    row = lax.broadcasted_iota(jnp.int32, lf.shape, 0)
    lane = lax.broadcasted_iota(jnp.int32, lf.shape, 1)
    f_cum = _chunk_cumsum(lf, row, tm) + carry_s[0:1, :]
    carry_s[...] = jnp.broadcast_to(f_cum[tm - 1:tm, :], carry_s.shape)
    fab_ref[...] = _dot(_slabs_with_one(f_cum, lane), sel_ref[...]).astype(BF16)

    width = q_ref.shape[1]
    scale = HEAD_DIM ** -0.5 * LOG2E
    for n0 in range(0, width, n_chunk):
        yq = _dot(h, wmain_ref[:, n0:n0 + n_chunk])
        yk = _dot(h, wmain_ref[:, width + n0:width + n0 + n_chunk])
        for c0 in range(0, n_chunk, HEAD_DIM):
            q_ref[:, n0 + c0:n0 + c0 + HEAD_DIM] = (
                _rms(yq[:, c0:c0 + HEAD_DIM], qnw_ref[...]) * scale).astype(BF16)
            k_ref[:, n0 + c0:n0 + c0 + HEAD_DIM] = _rms(
                yk[:, c0:c0 + HEAD_DIM], knw_ref[...]).astype(BF16)
        v_ref[:, n0:n0 + n_chunk] = _dot(
            h, wmain_ref[:, 2 * width + n0:2 * width + n0 + n_chunk]).astype(BF16)
        gate_ref[:, n0:n0 + n_chunk] = _dot(
            h, wmain_ref[:, 3 * width + n0:3 * width + n0 + n_chunk]).astype(BF16)


def _pad_lanes(v):
    return jnp.zeros((1, LANES), F32).at[0, :v.shape[0]].set(v.astype(F32))


def _pad_cols(w):
    return jnp.zeros((w.shape[0], LANES), w.dtype).at[:, :w.shape[1]].set(w)


def _in_proj_common_specs(tm, d, seq, n_mem, mw):
    assert seq % tm == 0, "a row block must not straddle two sequences"
    steps_per_seq = seq // tm
    x_spec = pl.BlockSpec((tm, d), lambda i: (i, 0))
    mem_spec = pl.BlockSpec((1, n_mem, mw), lambda i: (i // steps_per_seq, 0, 0))
    return x_spec, mem_spec


def _in_proj_dn(x2, seq, norm_w, w_in, a_log, dt_bias, memq_w, mem_k, mem_v, *, tm=1024):
    m, d = x2.shape
    tm = min(tm, seq)
    heads = a_log.shape[0]
    mix = heads * HEAD_DIM
    n_mem, mw = mem_k.shape[1], mem_k.shape[2]
    w_all = w_in[:, :4 * mix].astype(BF16)
    w_ab = _pad_cols(w_in[:, 4 * mix:4 * mix + 2 * heads]).astype(BF16)
    w_qm = w_in[:, 4 * mix + 2 * heads:].astype(BF16)
    x_spec, mem_spec = _in_proj_common_specs(tm, d, seq, n_mem, mw)
    row = lambda n: pl.BlockSpec((tm, n), lambda i: (i, 0))
    return pl.pallas_call(
        functools.partial(_in_proj_dn_kernel, n_chunk=512, heads=heads, chunk=DN_CHUNK),
        grid=(m // tm,),
        in_specs=[
            x_spec, _const_spec((1, d)), _const_spec(w_all.shape), _const_spec((d, LANES)),
            _const_spec((d, mw)), _const_spec((1, LANES)), _const_spec((1, LANES)),
            _const_spec((1, HEAD_DIM)), _const_spec((3 * LANES, 2 * LANES)), mem_spec, mem_spec,
        ],
        out_specs=[row(3 * mix), row(mix), row(3 * LANES), row(2 * LANES), row(mw)],
        out_shape=[
            jax.ShapeDtypeStruct((m, 3 * mix), BF16),
            jax.ShapeDtypeStruct((m, mix), BF16),
            jax.ShapeDtypeStruct((m, 3 * LANES), BF16),
            jax.ShapeDtypeStruct((m, 2 * LANES), BF16),
            jax.ShapeDtypeStruct((m, mw), BF16),
        ],
        compiler_params=pltpu.CompilerParams(
            dimension_semantics=("arbitrary",), vmem_limit_bytes=VMEM_LIMIT),
        name="in_proj_dn",
    )(x2, norm_w.reshape(1, d), w_all, w_ab, w_qm, _pad_lanes(a_log), _pad_lanes(dt_bias),
      memq_w.reshape(1, HEAD_DIM), _diff_selector(heads), mem_k, mem_v)


def _in_proj_fox(x2, seq, norm_w, w_in, f_bias, q_norm_w, k_norm_w, memq_w, mem_k, mem_v, *, tm=1024):
    m, d = x2.shape
    tm = min(tm, seq)
    heads = f_bias.shape[0]
    mix = heads * HEAD_DIM
    n_mem, mw = mem_k.shape[1], mem_k.shape[2]
    w_all = w_in[:, :4 * mix].astype(BF16)
    w_f = _pad_cols(w_in[:, 4 * mix:4 * mix + heads]).astype(BF16)
    w_qm = w_in[:, 4 * mix + heads:].astype(BF16)
    x_spec, mem_spec = _in_proj_common_specs(tm, d, seq, n_mem, mw)
    row = lambda n: pl.BlockSpec((tm, n), lambda i: (i, 0))
    return pl.pallas_call(
        functools.partial(_in_proj_fox_kernel, n_chunk=512, steps_per_seq=seq // tm),
        grid=(m // tm,),
        in_specs=[
            x_spec, _const_spec((1, d)), _const_spec(w_all.shape), _const_spec((d, LANES)),
            _const_spec((d, mw)), _const_spec((1, LANES)), _const_spec((1, HEAD_DIM)),
            _const_spec((1, HEAD_DIM)), _const_spec((1, HEAD_DIM)),
            _const_spec((3 * LANES, 2 * LANES)), mem_spec, mem_spec,
        ],
        out_specs=[row(mix), row(mix), row(mix), row(mix), row(2 * LANES), row(mw)],
        out_shape=[jax.ShapeDtypeStruct((m, mix), BF16)] * 4 + [
            jax.ShapeDtypeStruct((m, 2 * LANES), BF16),
            jax.ShapeDtypeStruct((m, mw), BF16),
        ],
        scratch_shapes=[pltpu.VMEM((8, LANES), F32)],
        compiler_params=pltpu.CompilerParams(
            dimension_semantics=("arbitrary",), vmem_limit_bytes=VMEM_LIMIT),
        name="in_proj_fox",
    )(x2, norm_w.reshape(1, d), w_all, w_f, w_qm, _pad_lanes(f_bias),
      q_norm_w.reshape(1, HEAD_DIM), k_norm_w.reshape(1, HEAD_DIM),
      memq_w.reshape(1, HEAD_DIM), _diff_selector(heads), mem_k, mem_v)


def _dn_kernel(q_ref, k_ref, v_ref, z_ref, gb3_ref, gab_ref, cwq_ref, cwk_ref, cwv_ref, onw_ref,
               o_ref, e_s, ext_s, kq_s, nc_s, op_s, eg_s, state_s, *, hb, chunk, group, heads):
    seq = q_ref.shape[0]
    n_chunks = seq // chunk
    rows = group * chunk
    hg = pl.program_id(1)

    er = lax.broadcasted_iota(jnp.int32, (3 * LANES, 2 * LANES), 0)
    ec = lax.broadcasted_iota(jnp.int32, (3 * LANES, 2 * LANES), 1)
    for j in range(hb):
        head = hg * hb + j
        want = jnp.where(ec < LANES, head, heads + head)
        e_s[j] = jnp.where(jnp.bitwise_and(er, LANES - 1) == want, 1.0, 0.0).astype(BF16)
        state_s[j] = jnp.zeros((HEAD_DIM, HEAD_DIM), F32)

    def conv_silu(ref, cw_ref, ext_ref, sl, r0, first):
        main = ref[pl.ds(r0, rows), sl].astype(F32)
        h0 = pl.multiple_of(jnp.maximum(r0 - BF16_ROWS, 0), BF16_ROWS)
        ext_ref[0:BF16_ROWS, :] = jnp.where(first, 0.0, ref[pl.ds(h0, BF16_ROWS), sl].astype(F32))
        ext_ref[BF16_ROWS:, :] = main
        cw = cw_ref[:, sl]
        acc = main * cw[CONV_WIDTH - 1:CONV_WIDTH, :]
        for s in range(1, CONV_WIDTH):
            acc = acc + ext_ref[BF16_ROWS - s:BF16_ROWS - s + rows, :] * cw[
                CONV_WIDTH - 1 - s:CONV_WIDTH - s, :]
        return acc * _sigmoid(acc)

    def l2n(x):
        return x * lax.rsqrt(jnp.sum(x * x, axis=-1, keepdims=True) + EPS)

    assert 2 * chunk == LANES
    ci = lax.broadcasted_iota(jnp.int32, (group, chunk, LANES), 1)
    cj = lax.broadcasted_iota(jnp.int32, (group, chunk, LANES), 2)
    causal = ci >= cj
    strict = ci > cj
    eye_right = jnp.where(cj - chunk == ci, 1.0, 0.0)
    lane2 = lax.broadcasted_iota(jnp.int32, (rows, LANES), 1)
    zeros_rows = jnp.zeros((group, chunk, LANES), BF16)
    nb = hb * group
    left = lax.broadcasted_iota(jnp.int32, (nb, chunk, LANES), 2) < chunk
    zeros_rows_all = jnp.zeros((nb, chunk, LANES), BF16)
    zeros_wide_all = jnp.zeros((nb, chunk, 2 * HEAD_DIM), BF16)
    n_double = chunk.bit_length() - 1
    shape3 = (group, chunk, HEAD_DIM)

    def bmm(a, b):
        return jnp.einsum("gij,gjk->gik", a, b, preferred_element_type=F32)

    def bmm_nt(a, b):
        return jnp.einsum("gid,gjd->gij", a, b, preferred_element_type=F32)

    def prep(gi, carry):
        r0 = pl.multiple_of(gi * rows, rows)
        c0 = pl.multiple_of(gi * group, group)
        first = gi == 0

        def front(j):
            sl = slice(j * HEAD_DIM, (j + 1) * HEAD_DIM)
            q = (l2n(conv_silu(q_ref, cwq_ref, ext_s.at[0], sl, r0, first))
                 * (HEAD_DIM ** -0.5)).reshape(shape3)
            k = l2n(conv_silu(k_ref, cwk_ref, ext_s.at[1], sl, r0, first)).reshape(shape3)
            v = conv_silu(v_ref, cwv_ref, ext_s.at[2], sl, r0, first).reshape(shape3)
            gbb = _dot(gb3_ref[pl.ds(r0, rows), :], e_s[j])
            g = gbb[:, :LANES].reshape(shape3)
            beta = gbb[:, LANES:].reshape(shape3)
            gab = gab_ref[pl.ds(r0, rows), :]
            own = jnp.right_shift(lane2, 3) == hg * hb + j
            ga = gab[:, :LANES].reshape(shape3)
            gbm = jnp.where(own, gab[:, LANES:], jnp.zeros_like(gab[:, LANES:])).reshape(shape3)
            diff = bmm_nt(ga, jnp.concatenate([gbm, zeros_rows], axis=1))
            decay = jnp.where(causal, jnp.exp(jnp.where(causal, diff, 0.0)), 0.0)
            kb = k * beta
            kq = jnp.concatenate([kb, q], axis=1).astype(BF16)
            aq = bmm_nt(kq, jnp.concatenate([k.astype(BF16), zeros_rows], axis=1))
            qk = (aq[:, chunk:] * decay)[:, :, :chunk].astype(BF16)
            y = eye_right - jnp.where(strict, aq[:, :chunk] * decay, 0.0)
            return q, k, v, g, beta, kb, qk, y

        def back(per_head):
            q, k, v, g, beta, kb, qk, y = [jnp.concatenate(t, axis=0) for t in zip(*per_head)]
            for _ in range(n_double):
                yb = y.astype(BF16)
                py = bmm(yb, jnp.concatenate([yb, zeros_rows_all], axis=1))
                y = jnp.where(left, py, y + py)
            eg = jnp.exp(g)
            rhs = jnp.concatenate([v * beta, kb * eg], axis=2).astype(BF16)
            x = bmm(y.astype(BF16), jnp.concatenate([zeros_wide_all, rhs], axis=1))
            xb = x.astype(BF16)
            qx = bmm(qk, xb)
            g_last = g[:, chunk - 1:chunk, :]
            kd = (k * jnp.exp(g_last - g)).astype(BF16)
            qp = (q * eg - qx[:, :, HEAD_DIM:]).astype(BF16)
            eg_last = jnp.broadcast_to(jnp.exp(g_last), (nb, 8, LANES))
            for j in range(hb):
                for c in range(group):
                    kx = _dot_tn(kd[j * group + c], xb[j * group + c])
                    nc_s[j, c0 + c] = kx[:, :HEAD_DIM]
                    kq_s[j, c0 + c, :HEAD_DIM, :] = kx[:, HEAD_DIM:].astype(BF16)
                mine = slice(j * group, (j + 1) * group)
                kq_s[j, pl.ds(c0, group), HEAD_DIM:, :] = qp[mine]
                op_s[j, pl.ds(c0, group)] = qx[mine, :, :HEAD_DIM]
                eg_s[j, pl.ds(c0, group)] = eg_last[mine]

        back([front(j) for j in range(hb)])
        return carry

    lax.fori_loop(0, n_chunks // group, prep, 0)

    def step(c, carry):
        r0 = pl.multiple_of(c * chunk, chunk)
        for j in range(hb):
            sl = slice(j * HEAD_DIM, (j + 1) * HEAD_DIM)
            state = state_s[j]
            sb = state.astype(BF16)
            ks = _dot(kq_s[j, c], sb)
            out = ks[HEAD_DIM:] + op_s[j, c]
            state_s[j] = state * eg_s[j, c][0:1, :] + nc_s[j, c] - ks[:HEAD_DIM]
            z = z_ref[pl.ds(r0, chunk), sl].astype(F32)
            o_ref[pl.ds(r0, chunk), sl] = (
                _rms(out, onw_ref[...]) * (z * _sigmoid(z))).astype(BF16)
        return carry

    lax.fori_loop(0, n_chunks, step, 0, unroll=STEP_UNROLL)


def _deltanet(qkv, z, gb3, gab, conv_w, o_norm_w, batch, seq, heads, *, hb=4, group=8):
    m = qkv.shape[0]
    mix = heads * HEAD_DIM
    chunk = DN_CHUNK
    group = min(group, seq // chunk)
    n_chunks = seq // chunk
    hgroups = heads // hb
    wblk = hb * HEAD_DIM
    col = lambda off: pl.BlockSpec((seq, wblk), lambda b, h: (b, off + h))
    cw = lambda off: pl.BlockSpec((CONV_WIDTH, wblk), lambda b, h: (0, off + h))
    return pl.pallas_call(
        functools.partial(_dn_kernel, hb=hb, chunk=chunk, group=group, heads=heads),
        grid=(batch, hgroups),
        in_specs=[
            col(0), col(hgroups), col(2 * hgroups),
            pl.BlockSpec((seq, wblk), lambda b, h: (b, h)),
            pl.BlockSpec((seq, 3 * LANES), lambda b, h: (b, 0)),
            pl.BlockSpec((seq, 2 * LANES), lambda b, h: (b, 0)),
            cw(0), cw(hgroups), cw(2 * hgroups),
            pl.BlockSpec((1, HEAD_DIM), lambda b, h: (0, 0)),
        ],
        out_specs=pl.BlockSpec((seq, wblk), lambda b, h: (b, h)),
        out_shape=jax.ShapeDtypeStruct((m, mix), BF16),
        scratch_shapes=[
            pltpu.VMEM((hb, 3 * LANES, 2 * LANES), BF16),
            pltpu.VMEM((3, group * chunk + BF16_ROWS, HEAD_DIM), F32),
            pltpu.VMEM((hb, n_chunks, HEAD_DIM + chunk, HEAD_DIM), BF16),
            pltpu.VMEM((hb, n_chunks, HEAD_DIM, HEAD_DIM), F32),
            pltpu.VMEM((hb, n_chunks, chunk, HEAD_DIM), F32),
            pltpu.VMEM((hb, n_chunks, 8, LANES), F32),
            pltpu.VMEM((hb, HEAD_DIM, HEAD_DIM), F32),
        ],
        compiler_params=pltpu.CompilerParams(
            dimension_semantics=("arbitrary", "arbitrary"), vmem_limit_bytes=VMEM_LIMIT),
        name="deltanet",
    )(qkv, qkv, qkv, z, gb3, gab, conv_w, conv_w, conv_w, o_norm_w.reshape(1, HEAD_DIM))


def _fox_kernel(q_ref, k_ref, v_ref, gate_ref, fab_ref, o_ref, qa_s, ka_s, vt_s, *, tq, hb):
    seq = q_ref.shape[0]
    n_q = seq // tq
    lane = lax.broadcasted_iota(jnp.int32, (seq, LANES), 1)
    key_le_query = (lax.broadcasted_iota(jnp.int32, (tq, tq), 0)
                    <= lax.broadcasted_iota(jnp.int32, (tq, tq), 1))

    def col_max(m, s):
        sm = jnp.max(s, axis=0, keepdims=True)
        return sm if m is None else jnp.maximum(m, sm)

    def head_program(j):
        sl = slice(j * HEAD_DIM, (j + 1) * HEAD_DIM)
        head = pl.program_id(1) * hb + j
        fb = fab_ref[:, LANES:]
        qa_s[j, :, :HEAD_DIM] = q_ref[:, sl]
        qa_s[j, :, HEAD_DIM:] = fab_ref[:, :LANES]
        ka_s[j, :, :HEAD_DIM] = k_ref[:, sl]
        ka_s[j, :, HEAD_DIM:] = jnp.where(jnp.right_shift(lane, 3) == head, fb, jnp.zeros_like(fb))
        vt_s[j] = v_ref[:, sl].astype(F32).T.astype(BF16)

        def score_tile(qi, kb):
            s = _dot_nt(ka_s[j, kb * tq:(kb + 1) * tq, :], qa_s[j, qi * tq:(qi + 1) * tq, :])
            return jnp.where(key_le_query, s, -jnp.inf) if kb == qi else s

        tiles = [score_tile(0, 0)]
        m = col_max(None, tiles[0])
        yield
        for qi in range(n_q):
            lo, hi = qi * tq, (qi + 1) * tq
            nxt_tiles, nxt_m, p_tiles, l = [], None, [], None
            n_next = qi + 2 if qi + 1 < n_q else 0
            for t in range(max(n_next, qi + 1)):
                if t < n_next:
                    nxt_tiles.append(score_tile(qi + 1, t))
                    nxt_m = col_max(nxt_m, nxt_tiles[-1])
                if t <= qi:
                    p = jnp.exp2(tiles[t] - m)
                    ps = jnp.sum(p, axis=0, keepdims=True)
                    l = ps if l is None else l + ps
                    p_tiles.append(p.astype(BF16))
            pt = p_tiles[0] if qi == 0 else jnp.concatenate(p_tiles, axis=0)
            ot = _dot(vt_s[j, :, :hi], pt)
            gate = gate_ref[lo:hi, sl].astype(F32)
            o_ref[lo:hi, sl] = ((ot / l).T * _sigmoid(gate)).astype(BF16)
            tiles, m = nxt_tiles, nxt_m
            yield

    programs = [head_program(j) for j in range(hb)]
    for _ in range(n_q + 1):
        for prog in programs:
            next(prog)


def _fox_attention(q, k, v, gate, fab, batch, seq, heads, *, tq=256, hb=4):
    m, mix = q.shape
    tq = min(tq, seq)
    col = pl.BlockSpec((seq, hb * HEAD_DIM), lambda b, h: (b, h))
    return pl.pallas_call(
        functools.partial(_fox_kernel, tq=tq, hb=hb),
        grid=(batch, heads // hb),
        in_specs=[col, col, col, col, pl.BlockSpec((seq, 2 * LANES), lambda b, h: (b, 0))],
        out_specs=col,
        out_shape=jax.ShapeDtypeStruct((m, mix), BF16),
        scratch_shapes=[
            pltpu.VMEM((hb, seq, 2 * HEAD_DIM), BF16),
            pltpu.VMEM((hb, seq, 2 * HEAD_DIM), BF16),
            pltpu.VMEM((hb, HEAD_DIM, seq), BF16),
        ],
        compiler_params=pltpu.CompilerParams(
            dimension_semantics=("arbitrary", "arbitrary"), vmem_limit_bytes=VMEM_LIMIT),
        name="fox_attention",
    )(q, k, v, gate, fab)


def _post_kernel(x_ref, mix_ref, mo_ref, wo_ref, nw_ref, w1_ref, w2_ref, y_ref, hid_s, *, f_chunk):
    mixw = mix_ref.shape[1]
    x1 = x_ref[...] + _dot(mix_ref[...], wo_ref[:mixw, :]) + _dot(mo_ref[...], wo_ref[mixw:, :])
    h = _rms(x1, nw_ref[...]).astype(BF16)
    for f0 in range(0, w1_ref.shape[1], f_chunk):
        t = jnp.maximum(_dot(h, w1_ref[:, f0:f0 + f_chunk]), 0.0)
        hid_s[:, f0:f0 + f_chunk] = (t * t).astype(BF16)
    y_ref[...] = x1 + _dot(hid_s[...], w2_ref[...])


def _post(x2, mix, mem_out, w_out, norm_w, w1, w2, layer, *, tm=512, f_chunk=512):
    m, d = x2.shape
    mixw, mw, dff = mix.shape[1], mem_out.shape[1], w1.shape[2]
    row = lambda n: pl.BlockSpec((tm, n), lambda i: (i, 0))
    layer_spec = lambda r, c: pl.BlockSpec(
        (None, r, c), lambda i: (layer, 0, 0), pipeline_mode=pl.Buffered(1))
    return pl.pallas_call(
        functools.partial(_post_kernel, f_chunk=f_chunk),
        grid=(m // tm,),
        in_specs=[
            row(d), row(mixw), row(mw), layer_spec(mixw + mw, d), _const_spec((1, d)),
            layer_spec(d, dff), layer_spec(dff, d),
        ],
        out_specs=row(d),
        out_shape=jax.ShapeDtypeStruct((m, d), F32),
        scratch_shapes=[pltpu.VMEM((tm, dff), BF16)],
        compiler_params=pltpu.CompilerParams(
            dimension_semantics=("arbitrary",), vmem_limit_bytes=VMEM_LIMIT),
        name="post",
    )(x2, mix, mem_out, w_out, norm_w.reshape(1, d), w1, w2)


def kernel(x, mem, mem_norm_w, w_mem_kv, mem_k_norm_w, norm1_w, dn_w_in, dn_conv_w, dn_a_log,
           dn_dt_bias, dn_o_norm_w, fox_w_in, fox_f_bias, fox_q_norm_w, fox_k_norm_w, memq_norm_w,
           w_out, norm2_w, w_mlp1, w_mlp2):
    batch, seq, d = x.shape
    depth = norm1_w.shape[0]
    mem_k, mem_v = _mem_kv(mem, mem_norm_w, w_mem_kv, mem_k_norm_w)
    x2 = x.reshape(batch * seq, d)
    w_out_b, w_mlp1_b, w_mlp2_b = w_out.astype(BF16), w_mlp1.astype(BF16), w_mlp2.astype(BF16)
    for i in range(depth):
        j = i // 2
        if i % 2 == 0:
            heads = dn_a_log.shape[1]
            qkv, z, gb3, gab, mem_out = _in_proj_dn(
                x2, seq, norm1_w[i], dn_w_in[j], dn_a_log[j], dn_dt_bias[j], memq_norm_w[i],
                mem_k, mem_v)
            mix = _deltanet(qkv, z, gb3, gab, dn_conv_w[j], dn_o_norm_w[j], batch, seq, heads)
        else:
            heads = fox_f_bias.shape[1]
            q, k, v, gate, fab, mem_out = _in_proj_fox(
                x2, seq, norm1_w[i], fox_w_in[j], fox_f_bias[j], fox_q_norm_w[j],
                fox_k_norm_w[j], memq_norm_w[i], mem_k, mem_v)
            mix = _fox_attention(q, k, v, gate, fab, batch, seq, heads)
        x2 = _post(x2, mix, mem_out, w_out_b, norm2_w[i], w_mlp1_b, w_mlp2_b, i)
    return x2.reshape(batch, seq, d)
```

```python
import functools

import jax
import jax.numpy as jnp
from jax import lax
from jax.experimental import pallas as pl
from jax.experimental.pallas import tpu as pltpu

F32 = jnp.float32
BF16 = jnp.bfloat16

HEAD_DIM = 128
LANES = 128
BF16_ROWS = 16
MEM_HEADS = 4
CONV_WIDTH = 4
EPS = 1e-6
DN_CHUNK = 64
STEP_UNROLL = 16
VMEM_LIMIT = 56 * 1024 * 1024
LOG2E = 1.4426950408889634
ONES_LANE = LANES - 1


def _sigmoid(x):
    return 1.0 / (1.0 + jnp.exp(-x))


def _softplus(x):
    return jnp.maximum(x, 0.0) + jnp.log1p(jnp.exp(-jnp.abs(x)))


def _rms(x, w):
    return x * lax.rsqrt(jnp.mean(x * x, axis=-1, keepdims=True) + EPS) * w


def _dot(a, b):
    return jnp.dot(a, b, preferred_element_type=F32)


def _dot_nt(a, b):
    return lax.dot_general(a, b, (((1,), (1,)), ((), ())), preferred_element_type=F32)


def _dot_tn(a, b):
    return lax.dot_general(a, b, (((0,), (0,)), ((), ())), preferred_element_type=F32)


def _split3(x):
    hi = x.astype(BF16).astype(F32)
    r = x - hi
    mid = r.astype(BF16).astype(F32)
    lo = (r - mid).astype(BF16).astype(F32)
    return hi, mid, lo


DIFF_LANES = 8


def _diff_selector(heads):
    assert heads * DIFF_LANES <= LANES and heads <= ONES_LANE
    s = [[0.0] * (2 * LANES) for _ in range(3 * LANES)]
    for h in range(heads):
        for part in range(3):
            s[part * LANES + h][DIFF_LANES * h + part] = 1.0
            s[ONES_LANE][DIFF_LANES * h + 3 + part] = 1.0
            s[ONES_LANE][LANES + DIFF_LANES * h + part] = 1.0
            s[part * LANES + h][LANES + DIFF_LANES * h + 3 + part] = -1.0
    return jnp.array(s, BF16)


def _slabs_with_one(x, lane):
    hi, mid, lo = _split3(x)
    spare = lane == ONES_LANE
    return jnp.concatenate([jnp.where(spare, 1.0, hi), jnp.where(spare, 0.0, mid),
                            jnp.where(spare, 0.0, lo)], axis=1).astype(BF16)


def _chunk_cumsum(x, row_in_chunk, length):
    s = 1
    while s < length:
        x = x + jnp.where(row_in_chunk >= s, pltpu.roll(x, s, 0), 0.0)
        s *= 2
    return x


def _const_spec(shape):
    return pl.BlockSpec(shape, lambda *_: (0,) * len(shape), pipeline_mode=pl.Buffered(1))


def _mem_kv_kernel(mem_ref, nw_ref, w_ref, knw_ref, k_ref, v_ref):
    h = _rms(mem_ref[0], nw_ref[...]).astype(BF16)
    kv = _dot(h, w_ref[...])
    mw = kv.shape[1] // 2
    for hd in range(mw // HEAD_DIM):
        sl = slice(hd * HEAD_DIM, (hd + 1) * HEAD_DIM)
        k_ref[0, :, sl] = _rms(kv[:, sl], knw_ref[...]).astype(BF16)
    v_ref[0] = kv[:, mw:].astype(BF16)


def _mem_kv(mem, mem_norm_w, w_mem_kv, mem_k_norm_w):
    b, n_mem, d = mem.shape
    mw = w_mem_kv.shape[1] // 2
    return pl.pallas_call(
        _mem_kv_kernel,
        grid=(b,),
        in_specs=[
            pl.BlockSpec((1, n_mem, d), lambda i: (i, 0, 0)),
            _const_spec((1, d)),
            _const_spec((d, 2 * mw)),
            _const_spec((1, HEAD_DIM)),
        ],
        out_specs=[
            pl.BlockSpec((1, n_mem, mw), lambda i: (i, 0, 0)),
            pl.BlockSpec((1, n_mem, mw), lambda i: (i, 0, 0)),
        ],
        out_shape=[jax.ShapeDtypeStruct((b, n_mem, mw), BF16)] * 2,
        compiler_params=pltpu.CompilerParams(
            dimension_semantics=("arbitrary",), vmem_limit_bytes=VMEM_LIMIT),
        name="mem_kv",
    )(mem, mem_norm_w.reshape(1, d), w_mem_kv.astype(BF16), mem_k_norm_w.reshape(1, HEAD_DIM))


def _mem_attention(qm, mqw, mk_ref, mv_ref, mo_ref):
    scale = HEAD_DIM ** -0.5
    for hd in range(MEM_HEADS):
        sl = slice(hd * HEAD_DIM, (hd + 1) * HEAD_DIM)
        q = (_rms(qm[:, sl], mqw) * scale).astype(BF16)
        s = _dot_nt(q, mk_ref[0, :, sl])
        p = jnp.exp(s - jnp.max(s, axis=1, keepdims=True))
        o = _dot(p.astype(BF16), mv_ref[0, :, sl]) / jnp.sum(p, axis=1, keepdims=True)
        mo_ref[:, sl] = o.astype(BF16)


def _in_proj_dn_kernel(x_ref, nw_ref, wmain_ref, wab_ref, wqm_ref, alog_ref, dtb_ref, mqw_ref,
                       sel_ref, mk_ref, mv_ref, qkv_ref, z_ref, gb3_ref, gab_ref, mo_ref,
                       *, n_chunk, heads, chunk):
    h = _rms(x_ref[...], nw_ref[...]).astype(BF16)
    _mem_attention(_dot(h, wqm_ref[...]), mqw_ref[...], mk_ref, mv_ref, mo_ref)
    ab = _dot(h, wab_ref[...])
    lane = lax.broadcasted_iota(jnp.int32, ab.shape, 1)
    row = lax.broadcasted_iota(jnp.int32, ab.shape, 0)
    g = -jnp.exp(alog_ref[...]) * _softplus(ab + dtb_ref[...])
    g_cum = _chunk_cumsum(g, jnp.bitwise_and(row, chunk - 1), chunk)
    slabs = _slabs_with_one(jnp.where(lane < heads, g_cum, _sigmoid(ab)), lane)
    gb3_ref[...] = slabs
    gab_ref[...] = _dot(slabs, sel_ref[...]).astype(BF16)
    qkv_w = qkv_ref.shape[1]
    for n0 in range(0, qkv_w, n_chunk):
        qkv_ref[:, n0:n0 + n_chunk] = _dot(h, wmain_ref[:, n0:n0 + n_chunk]).astype(BF16)
    for n0 in range(0, z_ref.shape[1], n_chunk):
        z_ref[:, n0:n0 + n_chunk] = _dot(
            h, wmain_ref[:, qkv_w + n0:qkv_w + n0 + n_chunk]).astype(BF16)


def _in_proj_fox_kernel(x_ref, nw_ref, wmain_ref, wf_ref, wqm_ref, fb_ref, qnw_ref, knw_ref,
                        mqw_ref, sel_ref, mk_ref, mv_ref, q_ref, k_ref, v_ref, gate_ref, fab_ref,
                        mo_ref, carry_s, *, n_chunk, steps_per_seq):
    h = _rms(x_ref[...], nw_ref[...]).astype(BF16)
    _mem_attention(_dot(h, wqm_ref[...]), mqw_ref[...], mk_ref, mv_ref, mo_ref)

    @pl.when(pl.program_id(0) % steps_per_seq == 0)
    def _():
        carry_s[...] = jnp.zeros_like(carry_s)

    lf = -_softplus(-(_dot(h, wf_ref[...]) + fb_ref[...])) * LOG2E
    tm = lf.shape[0]
    row = lax.broadcasted_iota(jnp.int32, lf.shape, 0)
    lane = lax.broadcasted_iota(jnp.int32, lf.shape, 1)
    f_cum = _chunk_cumsum(lf, row, tm) + carry_s[0:1, :]
    carry_s[...] = jnp.broadcast_to(f_cum[tm - 1:tm, :], carry_s.shape)
    fab_ref[...] = _dot(_slabs_with_one(f_cum, lane), sel_ref[...]).astype(BF16)

    width = q_ref.shape[1]
    scale = HEAD_DIM ** -0.5 * LOG2E
    for n0 in range(0, width, n_chunk):
        yq = _dot(h, wmain_ref[:, n0:n0 + n_chunk])
        yk = _dot(h, wmain_ref[:, width + n0:width + n0 + n_chunk])
        for c0 in range(0, n_chunk, HEAD_DIM):
            q_ref[:, n0 + c0:n0 + c0 + HEAD_DIM] = (
                _rms(yq[:, c0:c0 + HEAD_DIM], qnw_ref[...]) * scale).astype(BF16)
            k_ref[:, n0 + c0:n0 + c0 + HEAD_DIM] = _rms(
                yk[:, c0:c0 + HEAD_DIM], knw_ref[...]).astype(BF16)
        v_ref[:, n0:n0 + n_chunk] = _dot(
            h, wmain_ref[:, 2 * width + n0:2 * width + n0 + n_chunk]).astype(BF16)
        gate_ref[:, n0:n0 + n_chunk] = _dot(
            h, wmain_ref[:, 3 * width + n0:3 * width + n0 + n_chunk]).astype(BF16)


def _pad_lanes(v):
    return jnp.zeros((1, LANES), F32).at[0, :v.shape[0]].set(v.astype(F32))


def _pad_cols(w):
    return jnp.zeros((w.shape[0], LANES), w.dtype).at[:, :w.shape[1]].set(w)


def _in_proj_common_specs(tm, d, seq, n_mem, mw):
    assert seq % tm == 0, "a row block must not straddle two sequences"
    steps_per_seq = seq // tm
    x_spec = pl.BlockSpec((tm, d), lambda i: (i, 0))
    mem_spec = pl.BlockSpec((1, n_mem, mw), lambda i: (i // steps_per_seq, 0, 0))
    return x_spec, mem_spec


def _in_proj_dn(x2, seq, norm_w, w_in, a_log, dt_bias, memq_w, mem_k, mem_v, *, tm=1024):
    m, d = x2.shape
    tm = min(tm, seq)
    heads = a_log.shape[0]
    mix = heads * HEAD_DIM
    n_mem, mw = mem_k.shape[1], mem_k.shape[2]
    w_all = w_in.astype(BF16)
    w_ab = _pad_cols(w_in[:, 4 * mix:4 * mix + 2 * heads]).astype(BF16)
    w_qm = w_in[:, 4 * mix + 2 * heads:].astype(BF16)
    x_spec, mem_spec = _in_proj_common_specs(tm, d, seq, n_mem, mw)
    row = lambda n: pl.BlockSpec((tm, n), lambda i: (i, 0))
    return pl.pallas_call(
        functools.partial(_in_proj_dn_kernel, n_chunk=512, heads=heads, chunk=DN_CHUNK),
        grid=(m // tm,),
        in_specs=[
            x_spec, _const_spec((1, d)), _const_spec(w_all.shape), _const_spec((d, LANES)),
            _const_spec((d, mw)), _const_spec((1, LANES)), _const_spec((1, LANES)),
            _const_spec((1, HEAD_DIM)), _const_spec((3 * LANES, 2 * LANES)), mem_spec, mem_spec,
        ],
        out_specs=[row(3 * mix), row(mix), row(3 * LANES), row(2 * LANES), row(mw)],
        out_shape=[
            jax.ShapeDtypeStruct((m, 3 * mix), BF16),
            jax.ShapeDtypeStruct((m, mix), BF16),
            jax.ShapeDtypeStruct((m, 3 * LANES), BF16),
            jax.ShapeDtypeStruct((m, 2 * LANES), BF16),
            jax.ShapeDtypeStruct((m, mw), BF16),
        ],
        compiler_params=pltpu.CompilerParams(
            dimension_semantics=("arbitrary",), vmem_limit_bytes=VMEM_LIMIT),
        name="in_proj_dn",
    )(x2, norm_w.reshape(1, d), w_all, w_ab, w_qm, _pad_lanes(a_log), _pad_lanes(dt_bias),
      memq_w.reshape(1, HEAD_DIM), _diff_selector(heads), mem_k, mem_v)


def _in_proj_fox(x2, seq, norm_w, w_in, f_bias, q_norm_w, k_norm_w, memq_w, mem_k, mem_v, *, tm=1024):
    m, d = x2.shape
    tm = min(tm, seq)
    heads = f_bias.shape[0]
    mix = heads * HEAD_DIM
    n_mem, mw = mem_k.shape[1], mem_k.shape[2]
    w_all = w_in.astype(BF16)
    w_f = _pad_cols(w_in[:, 4 * mix:4 * mix + heads]).astype(BF16)
    w_qm = w_in[:, 4 * mix + heads:].astype(BF16)
    x_spec, mem_spec = _in_proj_common_specs(tm, d, seq, n_mem, mw)
    row = lambda n: pl.BlockSpec((tm, n), lambda i: (i, 0))
    return pl.pallas_call(
        functools.partial(_in_proj_fox_kernel, n_chunk=512, steps_per_seq=seq // tm),
        grid=(m // tm,),
        in_specs=[
            x_spec, _const_spec((1, d)), _const_spec(w_all.shape), _const_spec((d, LANES)),
            _const_spec((d, mw)), _const_spec((1, LANES)), _const_spec((1, HEAD_DIM)),
            _const_spec((1, HEAD_DIM)), _const_spec((1, HEAD_DIM)),
            _const_spec((3 * LANES, 2 * LANES)), mem_spec, mem_spec,
        ],
        out_specs=[row(mix), row(mix), row(mix), row(mix), row(2 * LANES), row(mw)],
        out_shape=[jax.ShapeDtypeStruct((m, mix), BF16)] * 4 + [
            jax.ShapeDtypeStruct((m, 2 * LANES), BF16),
            jax.ShapeDtypeStruct((m, mw), BF16),
        ],
        scratch_shapes=[pltpu.VMEM((8, LANES), F32)],
        compiler_params=pltpu.CompilerParams(
            dimension_semantics=("arbitrary",), vmem_limit_bytes=VMEM_LIMIT),
        name="in_proj_fox",
    )(x2, norm_w.reshape(1, d), w_all, w_f, w_qm, _pad_lanes(f_bias),
      q_norm_w.reshape(1, HEAD_DIM), k_norm_w.reshape(1, HEAD_DIM),
      memq_w.reshape(1, HEAD_DIM), _diff_selector(heads), mem_k, mem_v)


def _dn_kernel(q_ref, k_ref, v_ref, z_ref, gb3_ref, gab_ref, cwq_ref, cwk_ref, cwv_ref, onw_ref,
               o_ref, e_s, ext_s, kq_s, nc_s, op_s, eg_s, state_s, *, hb, chunk, group, heads):
    seq = q_ref.shape[0]
    n_chunks = seq // chunk
    rows = group * chunk
    hg = pl.program_id(1)

    er = lax.broadcasted_iota(jnp.int32, (3 * LANES, 2 * LANES), 0)
    ec = lax.broadcasted_iota(jnp.int32, (3 * LANES, 2 * LANES), 1)
    for j in range(hb):
        head = hg * hb + j
        want = jnp.where(ec < LANES, head, heads + head)
        e_s[j] = jnp.where(jnp.bitwise_and(er, LANES - 1) == want, 1.0, 0.0).astype(BF16)
        state_s[j] = jnp.zeros((HEAD_DIM, HEAD_DIM), F32)

    def conv_silu(ref, cw_ref, ext_ref, sl, r0, first):
        main = ref[pl.ds(r0, rows), sl].astype(F32)
        h0 = pl.multiple_of(jnp.maximum(r0 - BF16_ROWS, 0), BF16_ROWS)
        ext_ref[0:BF16_ROWS, :] = jnp.where(first, 0.0, ref[pl.ds(h0, BF16_ROWS), sl].astype(F32))
        ext_ref[BF16_ROWS:, :] = main
        cw = cw_ref[:, sl]
        acc = main * cw[CONV_WIDTH - 1:CONV_WIDTH, :]
        for s in range(1, CONV_WIDTH):
            acc = acc + ext_ref[BF16_ROWS - s:BF16_ROWS - s + rows, :] * cw[
                CONV_WIDTH - 1 - s:CONV_WIDTH - s, :]
        return acc * _sigmoid(acc)

    def l2n(x):
        return x * lax.rsqrt(jnp.sum(x * x, axis=-1, keepdims=True) + EPS)

    assert 2 * chunk == LANES
    ci = lax.broadcasted_iota(jnp.int32, (group, chunk, LANES), 1)
    cj = lax.broadcasted_iota(jnp.int32, (group, chunk, LANES), 2)
    causal = ci >= cj
    strict = ci > cj
    eye_right = jnp.where(cj - chunk == ci, 1.0, 0.0)
    lane2 = lax.broadcasted_iota(jnp.int32, (rows, LANES), 1)
    zeros_rows = jnp.zeros((group, chunk, LANES), BF16)
    nb = hb * group
    left = lax.broadcasted_iota(jnp.int32, (nb, chunk, LANES), 2) < chunk
    zeros_rows_all = jnp.zeros((nb, chunk, LANES), BF16)
    zeros_wide_all = jnp.zeros((nb, chunk, 2 * HEAD_DIM), BF16)
    n_double = chunk.bit_length() - 1
    shape3 = (group, chunk, HEAD_DIM)

    def bmm(a, b):
        return jnp.einsum("gij,gjk->gik", a, b, preferred_element_type=F32)

    def bmm_nt(a, b):
        return jnp.einsum("gid,gjd->gij", a, b, preferred_element_type=F32)

    def prep(gi, carry):
        r0 = pl.multiple_of(gi * rows, rows)
        c0 = pl.multiple_of(gi * group, group)
        first = gi == 0

        def front(j):
            sl = slice(j * HEAD_DIM, (j + 1) * HEAD_DIM)
            q = (l2n(conv_silu(q_ref, cwq_ref, ext_s.at[0], sl, r0, first))
                 * (HEAD_DIM ** -0.5)).reshape(shape3)
            k = l2n(conv_silu(k_ref, cwk_ref, ext_s.at[1], sl, r0, first)).reshape(shape3)
            v = conv_silu(v_ref, cwv_ref, ext_s.at[2], sl, r0, first).reshape(shape3)
            gbb = _dot(gb3_ref[pl.ds(r0, rows), :], e_s[j])
            g = gbb[:, :LANES].reshape(shape3)
            beta = gbb[:, LANES:].reshape(shape3)
            gab = gab_ref[pl.ds(r0, rows), :]
            own = jnp.right_shift(lane2, 3) == hg * hb + j
            ga = gab[:, :LANES].reshape(shape3)
            gbm = jnp.where(own, gab[:, LANES:], jnp.zeros_like(gab[:, LANES:])).reshape(shape3)
            diff = bmm_nt(ga, jnp.concatenate([gbm, zeros_rows], axis=1))
            decay = jnp.where(causal, jnp.exp(jnp.where(causal, diff, 0.0)), 0.0)
            kb = k * beta
            kq = jnp.concatenate([kb, q], axis=1).astype(BF16)
            aq = bmm_nt(kq, jnp.concatenate([k.astype(BF16), zeros_rows], axis=1))
            qk = (aq[:, chunk:] * decay)[:, :, :chunk].astype(BF16)
            y = eye_right - jnp.where(strict, aq[:, :chunk] * decay, 0.0)
            return q, k, v, g, beta, kb, qk, y

        def back(per_head):
            q, k, v, g, beta, kb, qk, y = [jnp.concatenate(t, axis=0) for t in zip(*per_head)]
            for _ in range(n_double):
                yb = y.astype(BF16)
                py = bmm(yb, jnp.concatenate([yb, zeros_rows_all], axis=1))
                y = jnp.where(left, py, y + py)
            eg = jnp.exp(g)
            rhs = jnp.concatenate([v * beta, kb * eg], axis=2).astype(BF16)
            x = bmm(y.astype(BF16), jnp.concatenate([zeros_wide_all, rhs], axis=1))
            xb = x.astype(BF16)
            qx = bmm(qk, xb)
            g_last = g[:, chunk - 1:chunk, :]
            kd = (k * jnp.exp(g_last - g)).astype(BF16)
            qp = (q * eg - qx[:, :, HEAD_DIM:]).astype(BF16)
            eg_last = jnp.broadcast_to(jnp.exp(g_last), (nb, 8, LANES))
            for j in range(hb):
                for c in range(group):
                    kx = _dot_tn(kd[j * group + c], xb[j * group + c])
                    nc_s[j, c0 + c] = kx[:, :HEAD_DIM]
                    kq_s[j, c0 + c, :HEAD_DIM, :] = kx[:, HEAD_DIM:].astype(BF16)
                mine = slice(j * group, (j + 1) * group)
                kq_s[j, pl.ds(c0, group), HEAD_DIM:, :] = qp[mine]
                op_s[j, pl.ds(c0, group)] = qx[mine, :, :HEAD_DIM]
                eg_s[j, pl.ds(c0, group)] = eg_last[mine]

        back([front(j) for j in range(hb)])
        return carry

    lax.fori_loop(0, n_chunks // group, prep, 0)

    def step(c, carry):
        r0 = pl.multiple_of(c * chunk, chunk)
        for j in range(hb):
            sl = slice(j * HEAD_DIM, (j + 1) * HEAD_DIM)
            state = state_s[j]
            sb = state.astype(BF16)
            ks = _dot(kq_s[j, c], sb)
            out = ks[HEAD_DIM:] + op_s[j, c]
            state_s[j] = state * eg_s[j, c][0:1, :] + nc_s[j, c] - ks[:HEAD_DIM]
            z = z_ref[pl.ds(r0, chunk), sl].astype(F32)
            o_ref[pl.ds(r0, chunk), sl] = (
                _rms(out, onw_ref[...]) * (z * _sigmoid(z))).astype(BF16)
        return carry

    lax.fori_loop(0, n_chunks, step, 0, unroll=STEP_UNROLL)


def _deltanet(qkv, z, gb3, gab, conv_w, o_norm_w, batch, seq, heads, *, hb=4, group=8):
    m = qkv.shape[0]
    mix = heads * HEAD_DIM
    chunk = DN_CHUNK
    group = min(group, seq // chunk)
    n_chunks = seq // chunk
    hgroups = heads // hb
    wblk = hb * HEAD_DIM
    col = lambda off: pl.BlockSpec((seq, wblk), lambda b, h: (b, off + h))
    cw = lambda off: pl.BlockSpec((CONV_WIDTH, wblk), lambda b, h: (0, off + h))
    return pl.pallas_call(
        functools.partial(_dn_kernel, hb=hb, chunk=chunk, group=group, heads=heads),
        grid=(batch, hgroups),
        in_specs=[
            col(0), col(hgroups), col(2 * hgroups),
            pl.BlockSpec((seq, wblk), lambda b, h: (b, h)),
            pl.BlockSpec((seq, 3 * LANES), lambda b, h: (b, 0)),
            pl.BlockSpec((seq, 2 * LANES), lambda b, h: (b, 0)),
            cw(0), cw(hgroups), cw(2 * hgroups),
            pl.BlockSpec((1, HEAD_DIM), lambda b, h: (0, 0)),
        ],
        out_specs=pl.BlockSpec((seq, wblk), lambda b, h: (b, h)),
        out_shape=jax.ShapeDtypeStruct((m, mix), BF16),
        scratch_shapes=[
            pltpu.VMEM((hb, 3 * LANES, 2 * LANES), BF16),
            pltpu.VMEM((3, group * chunk + BF16_ROWS, HEAD_DIM), F32),
            pltpu.VMEM((hb, n_chunks, HEAD_DIM + chunk, HEAD_DIM), BF16),
            pltpu.VMEM((hb, n_chunks, HEAD_DIM, HEAD_DIM), F32),
            pltpu.VMEM((hb, n_chunks, chunk, HEAD_DIM), F32),
            pltpu.VMEM((hb, n_chunks, 8, LANES), F32),
            pltpu.VMEM((hb, HEAD_DIM, HEAD_DIM), F32),
        ],
        compiler_params=pltpu.CompilerParams(
            dimension_semantics=("arbitrary", "arbitrary"), vmem_limit_bytes=VMEM_LIMIT),
        name="deltanet",
    )(qkv, qkv, qkv, z, gb3, gab, conv_w, conv_w, conv_w, o_norm_w.reshape(1, HEAD_DIM))


def _fox_kernel(q_ref, k_ref, v_ref, gate_ref, fab_ref, o_ref, qa_s, ka_s, vt_s, *, tq, hb):
    seq = q_ref.shape[0]
    n_q = seq // tq
    lane = lax.broadcasted_iota(jnp.int32, (seq, LANES), 1)
    key_le_query = (lax.broadcasted_iota(jnp.int32, (tq, tq), 0)
                    <= lax.broadcasted_iota(jnp.int32, (tq, tq), 1))

    def col_max(m, s):
        sm = jnp.max(s, axis=0, keepdims=True)
        return sm if m is None else jnp.maximum(m, sm)

    def head_program(j):
        sl = slice(j * HEAD_DIM, (j + 1) * HEAD_DIM)
        head = pl.program_id(1) * hb + j
        fb = fab_ref[:, LANES:]
        qa_s[j, :, :HEAD_DIM] = q_ref[:, sl]
        qa_s[j, :, HEAD_DIM:] = fab_ref[:, :LANES]
        ka_s[j, :, :HEAD_DIM] = k_ref[:, sl]
        ka_s[j, :, HEAD_DIM:] = jnp.where(jnp.right_shift(lane, 3) == head, fb, jnp.zeros_like(fb))
        vt_s[j, :HEAD_DIM, :] = v_ref[:, sl].astype(F32).T.astype(BF16)
        vt_s[j, HEAD_DIM:, :] = jnp.where(
            lax.broadcasted_iota(jnp.int32, (BF16_ROWS, seq), 0) == 0, 1.0, 0.0).astype(BF16)

        def score_tile(qi, kb):
            s = _dot_nt(ka_s[j, kb * tq:(kb + 1) * tq, :], qa_s[j, qi * tq:(qi + 1) * tq, :])
            return jnp.where(key_le_query, s, -jnp.inf) if kb == qi else s

        tiles = [score_tile(0, 0)]
        m = col_max(None, tiles[0])
        yield
        for qi in range(n_q):
            lo, hi = qi * tq, (qi + 1) * tq
            nxt_tiles, nxt_m, p_tiles = [], None, []
            n_next = qi + 2 if qi + 1 < n_q else 0
            for t in range(max(n_next, qi + 1)):
                if t < n_next:
                    nxt_tiles.append(score_tile(qi + 1, t))
                    nxt_m = col_max(nxt_m, nxt_tiles[-1])
                if t <= qi:
                    p_tiles.append(jnp.exp2(tiles[t] - m).astype(BF16))
            pt = p_tiles[0] if qi == 0 else jnp.concatenate(p_tiles, axis=0)
            ot = _dot(vt_s[j, :, :hi], pt)
            gate = gate_ref[lo:hi, sl].astype(F32)
            o = ot[:HEAD_DIM] / ot[HEAD_DIM:HEAD_DIM + 1]
            o_ref[lo:hi, sl] = (o.T * _sigmoid(gate)).astype(BF16)
            tiles, m = nxt_tiles, nxt_m
            yield

    programs = [head_program(j) for j in range(hb)]
    for _ in range(n_q + 1):
        for prog in programs:
            next(prog)


def _fox_attention(q, k, v, gate, fab, batch, seq, heads, *, tq=256, hb=4):
    m, mix = q.shape
    tq = min(tq, seq)
    col = pl.BlockSpec((seq, hb * HEAD_DIM), lambda b, h: (b, h))
    return pl.pallas_call(
        functools.partial(_fox_kernel, tq=tq, hb=hb),
        grid=(batch, heads // hb),
        in_specs=[col, col, col, col, pl.BlockSpec((seq, 2 * LANES), lambda b, h: (b, 0))],
        out_specs=col,
        out_shape=jax.ShapeDtypeStruct((m, mix), BF16),
        scratch_shapes=[
            pltpu.VMEM((hb, seq, 2 * HEAD_DIM), BF16),
            pltpu.VMEM((hb, seq, 2 * HEAD_DIM), BF16),
            pltpu.VMEM((hb, HEAD_DIM + BF16_ROWS, seq), BF16),
        ],
        compiler_params=pltpu.CompilerParams(
            dimension_semantics=("arbitrary", "arbitrary"), vmem_limit_bytes=VMEM_LIMIT),
        name="fox_attention",
    )(q, k, v, gate, fab)


def _post_kernel(x_ref, mix_ref, mo_ref, wo_ref, nw_ref, w1_ref, w2_ref, y_ref, hid_s, *, f_chunk):
    mixw = mix_ref.shape[1]
    x1 = x_ref[...] + _dot(mix_ref[...], wo_ref[:mixw, :]) + _dot(mo_ref[...], wo_ref[mixw:, :])
    h = _rms(x1, nw_ref[...]).astype(BF16)
    for f0 in range(0, w1_ref.shape[1], f_chunk):
        t = jnp.maximum(_dot(h, w1_ref[:, f0:f0 + f_chunk]), 0.0)
        hid_s[:, f0:f0 + f_chunk] = (t * t).astype(BF16)
    y_ref[...] = x1 + _dot(hid_s[...], w2_ref[...])


def _post(x2, mix, mem_out, w_out, norm_w, w1, w2, layer, *, tm=512, f_chunk=512):
    m, d = x2.shape
    mixw, mw, dff = mix.shape[1], mem_out.shape[1], w1.shape[2]
    row = lambda n: pl.BlockSpec((tm, n), lambda i: (i, 0))
    layer_spec = lambda r, c: pl.BlockSpec(
        (None, r, c), lambda i: (layer, 0, 0), pipeline_mode=pl.Buffered(1))
    return pl.pallas_call(
        functools.partial(_post_kernel, f_chunk=f_chunk),
        grid=(m // tm,),
        in_specs=[
            row(d), row(mixw), row(mw), layer_spec(mixw + mw, d), _const_spec((1, d)),
            layer_spec(d, dff), layer_spec(dff, d),
        ],
        out_specs=row(d),
        out_shape=jax.ShapeDtypeStruct((m, d), F32),
        scratch_shapes=[pltpu.VMEM((tm, dff), BF16)],
        compiler_params=pltpu.CompilerParams(
            dimension_semantics=("arbitrary",), vmem_limit_bytes=VMEM_LIMIT),
        name="post",
    )(x2, mix, mem_out, w_out, norm_w.reshape(1, d), w1, w2)


def kernel(x, mem, mem_norm_w, w_mem_kv, mem_k_norm_w, norm1_w, dn_w_in, dn_conv_w, dn_a_log,
           dn_dt_bias, dn_o_norm_w, fox_w_in, fox_f_bias, fox_q_norm_w, fox_k_norm_w, memq_norm_w,
           w_out, norm2_w, w_mlp1, w_mlp2):
    batch, seq, d = x.shape
    depth = norm1_w.shape[0]
    mem_k, mem_v = _mem_kv(mem, mem_norm_w, w_mem_kv, mem_k_norm_w)
    x2 = x.reshape(batch * seq, d)
    w_out_b, w_mlp1_b, w_mlp2_b = w_out.astype(BF16), w_mlp1.astype(BF16), w_mlp2.astype(BF16)
    for i in range(depth):
        j = i // 2
        if i % 2 == 0:
            heads = dn_a_log.shape[1]
            qkv, z, gb3, gab, mem_out = _in_proj_dn(
                x2, seq, norm1_w[i], dn_w_in[j], dn_a_log[j], dn_dt_bias[j], memq_norm_w[i],
                mem_k, mem_v)
            mix = _deltanet(qkv, z, gb3, gab, dn_conv_w[j], dn_o_norm_w[j], batch, seq, heads)
        else:
            heads = fox_f_bias.shape[1]
            q, k, v, gate, fab, mem_out = _in_proj_fox(
                x2, seq, norm1_w[i], fox_w_in[j], fox_f_bias[j], fox_q_norm_w[j],
                fox_k_norm_w[j], memq_norm_w[i], mem_k, mem_v)
            mix = _fox_attention(q, k, v, gate, fab, batch, seq, heads)
        x2 = _post(x2, mix, mem_out, w_out_b, norm2_w[i], w_mlp1_b, w_mlp2_b, i)
    return x2.reshape(batch, seq, d)
```

```python
import functools

import jax
import jax.numpy as jnp
from jax import lax
from jax.experimental import pallas as pl
from jax.experimental.pallas import tpu as pltpu

F32 = jnp.float32
BF16 = jnp.bfloat16

HEAD_DIM = 128
LANES = 128
BF16_ROWS = 16
MEM_HEADS = 4
CONV_WIDTH = 4
EPS = 1e-6
DN_CHUNK = 64
STEP_UNROLL = 16
VMEM_LIMIT = 56 * 1024 * 1024
LOG2E = 1.4426950408889634
ONES_LANE = LANES - 1


def _sigmoid(x):
    return 1.0 / (1.0 + jnp.exp(-x))


def _softplus(x):
    return jnp.maximum(x, 0.0) + jnp.log1p(jnp.exp(-jnp.abs(x)))


def _rms(x, w):
    return x * lax.rsqrt(jnp.mean(x * x, axis=-1, keepdims=True) + EPS) * w


def _dot(a, b):
    return jnp.dot(a, b, preferred_element_type=F32)


def _dot_nt(a, b):
    return lax.dot_general(a, b, (((1,), (1,)), ((), ())), preferred_element_type=F32)


def _dot_tn(a, b):
    return lax.dot_general(a, b, (((0,), (0,)), ((), ())), preferred_element_type=F32)


def _split3(x):
    hi = x.astype(BF16).astype(F32)
    r = x - hi
    mid = r.astype(BF16).astype(F32)
    lo = (r - mid).astype(BF16).astype(F32)
    return hi, mid, lo


DIFF_LANES = 8


def _diff_selector(heads):
    assert heads * DIFF_LANES <= LANES and heads <= ONES_LANE
    s = [[0.0] * (2 * LANES) for _ in range(3 * LANES)]
    for h in range(heads):
        for part in range(3):
            s[part * LANES + h][DIFF_LANES * h + part] = 1.0
            s[ONES_LANE][DIFF_LANES * h + 3 + part] = 1.0
            s[ONES_LANE][LANES + DIFF_LANES * h + part] = 1.0
            s[part * LANES + h][LANES + DIFF_LANES * h + 3 + part] = -1.0
    return jnp.array(s, BF16)


def _slabs_with_one(x, lane):
    hi, mid, lo = _split3(x)
    spare = lane == ONES_LANE
    return jnp.concatenate([jnp.where(spare, 1.0, hi), jnp.where(spare, 0.0, mid),
                            jnp.where(spare, 0.0, lo)], axis=1).astype(BF16)


def _chunk_cumsum(x, row_in_chunk, length):
    s = 1
    while s < length:
        x = x + jnp.where(row_in_chunk >= s, pltpu.roll(x, s, 0), 0.0)
        s *= 2
    return x


def _const_spec(shape):
    return pl.BlockSpec(shape, lambda *_: (0,) * len(shape), pipeline_mode=pl.Buffered(1))


def _mem_kv_kernel(mem_ref, nw_ref, w_ref, knw_ref, k_ref, v_ref):
    h = _rms(mem_ref[0], nw_ref[...]).astype(BF16)
    kv = _dot(h, w_ref[...])
    mw = kv.shape[1] // 2
    for hd in range(mw // HEAD_DIM):
        sl = slice(hd * HEAD_DIM, (hd + 1) * HEAD_DIM)
        k_ref[0, :, sl] = _rms(kv[:, sl], knw_ref[...]).astype(BF16)
    v_ref[0] = kv[:, mw:].astype(BF16)


def _mem_kv(mem, mem_norm_w, w_mem_kv, mem_k_norm_w):
    b, n_mem, d = mem.shape
    mw = w_mem_kv.shape[1] // 2
    return pl.pallas_call(
        _mem_kv_kernel,
        grid=(b,),
        in_specs=[
            pl.BlockSpec((1, n_mem, d), lambda i: (i, 0, 0)),
            _const_spec((1, d)),
            _const_spec((d, 2 * mw)),
            _const_spec((1, HEAD_DIM)),
        ],
        out_specs=[
            pl.BlockSpec((1, n_mem, mw), lambda i: (i, 0, 0)),
            pl.BlockSpec((1, n_mem, mw), lambda i: (i, 0, 0)),
        ],
        out_shape=[jax.ShapeDtypeStruct((b, n_mem, mw), BF16)] * 2,
        compiler_params=pltpu.CompilerParams(
            dimension_semantics=("arbitrary",), vmem_limit_bytes=VMEM_LIMIT),
        name="mem_kv",
    )(mem, mem_norm_w.reshape(1, d), w_mem_kv.astype(BF16), mem_k_norm_w.reshape(1, HEAD_DIM))


def _mem_attention(qm, mqw, mk_ref, mv_ref, mo_ref):
    scale = HEAD_DIM ** -0.5
    for hd in range(MEM_HEADS):
        sl = slice(hd * HEAD_DIM, (hd + 1) * HEAD_DIM)
        q = (_rms(qm[:, sl], mqw) * scale).astype(BF16)
        s = _dot_nt(q, mk_ref[0, :, sl])
        p = jnp.exp(s - jnp.max(s, axis=1, keepdims=True))
        o = _dot(p.astype(BF16), mv_ref[0, :, sl]) / jnp.sum(p, axis=1, keepdims=True)
        mo_ref[:, sl] = o.astype(BF16)


def _in_proj_dn_kernel(x_ref, nw_ref, wmain_ref, wab_ref, wqm_ref, alog_ref, dtb_ref, mqw_ref,
                       sel_ref, mk_ref, mv_ref, qkv_ref, z_ref, gb3_ref, gab_ref, mo_ref,
                       *, n_chunk, heads, chunk):
    h = _rms(x_ref[...], nw_ref[...]).astype(BF16)
    _mem_attention(_dot(h, wqm_ref[...]), mqw_ref[...], mk_ref, mv_ref, mo_ref)
    ab = _dot(h, wab_ref[...])
    lane = lax.broadcasted_iota(jnp.int32, ab.shape, 1)
    row = lax.broadcasted_iota(jnp.int32, ab.shape, 0)
    g = -jnp.exp(alog_ref[...]) * _softplus(ab + dtb_ref[...])
    g_cum = _chunk_cumsum(g, jnp.bitwise_and(row, chunk - 1), chunk)
    slabs = _slabs_with_one(jnp.where(lane < heads, g_cum, _sigmoid(ab)), lane)
    gb3_ref[...] = slabs
    gab_ref[...] = _dot(slabs, sel_ref[...]).astype(BF16)
    qkv_w = qkv_ref.shape[1]
    for n0 in range(0, qkv_w, n_chunk):
        qkv_ref[:, n0:n0 + n_chunk] = _dot(h, wmain_ref[:, n0:n0 + n_chunk]).astype(BF16)
    for n0 in range(0, z_ref.shape[1], n_chunk):
        zc = _dot(h, wmain_ref[:, qkv_w + n0:qkv_w + n0 + n_chunk])
        z_ref[:, n0:n0 + n_chunk] = (zc * _sigmoid(zc)).astype(BF16)


def _in_proj_fox_kernel(x_ref, nw_ref, wmain_ref, wf_ref, wqm_ref, fb_ref, qnw_ref, knw_ref,
                        mqw_ref, sel_ref, mk_ref, mv_ref, q_ref, k_ref, v_ref, gate_ref, fab_ref,
                        mo_ref, carry_s, *, n_chunk, steps_per_seq):
    h = _rms(x_ref[...], nw_ref[...]).astype(BF16)
    _mem_attention(_dot(h, wqm_ref[...]), mqw_ref[...], mk_ref, mv_ref, mo_ref)

    @pl.when(pl.program_id(0) % steps_per_seq == 0)
    def _():
        carry_s[...] = jnp.zeros_like(carry_s)

    lf = -_softplus(-(_dot(h, wf_ref[...]) + fb_ref[...])) * LOG2E
    tm = lf.shape[0]
    row = lax.broadcasted_iota(jnp.int32, lf.shape, 0)
    lane = lax.broadcasted_iota(jnp.int32, lf.shape, 1)
    f_cum = _chunk_cumsum(lf, row, tm) + carry_s[0:1, :]
    carry_s[...] = jnp.broadcast_to(f_cum[tm - 1:tm, :], carry_s.shape)
    fab_ref[...] = _dot(_slabs_with_one(f_cum, lane), sel_ref[...]).astype(BF16)

    width = q_ref.shape[1]
    scale = HEAD_DIM ** -0.5 * LOG2E
    for n0 in range(0, width, n_chunk):
        yq = _dot(h, wmain_ref[:, n0:n0 + n_chunk])
        yk = _dot(h, wmain_ref[:, width + n0:width + n0 + n_chunk])
        for c0 in range(0, n_chunk, HEAD_DIM):
            q_ref[:, n0 + c0:n0 + c0 + HEAD_DIM] = (
                _rms(yq[:, c0:c0 + HEAD_DIM], qnw_ref[...]) * scale).astype(BF16)
            k_ref[:, n0 + c0:n0 + c0 + HEAD_DIM] = _rms(
                yk[:, c0:c0 + HEAD_DIM], knw_ref[...]).astype(BF16)
        v_ref[:, n0:n0 + n_chunk] = _dot(
            h, wmain_ref[:, 2 * width + n0:2 * width + n0 + n_chunk]).astype(BF16)
        gate_ref[:, n0:n0 + n_chunk] = _sigmoid(_dot(
            h, wmain_ref[:, 3 * width + n0:3 * width + n0 + n_chunk])).astype(BF16)


def _pad_lanes(v):
    return jnp.zeros((1, LANES), F32).at[0, :v.shape[0]].set(v.astype(F32))


def _pad_cols(w):
    return jnp.zeros((w.shape[0], LANES), w.dtype).at[:, :w.shape[1]].set(w)


def _in_proj_common_specs(tm, d, seq, n_mem, mw):
    assert seq % tm == 0, "a row block must not straddle two sequences"
    steps_per_seq = seq // tm
    x_spec = pl.BlockSpec((tm, d), lambda i: (i, 0))
    mem_spec = pl.BlockSpec((1, n_mem, mw), lambda i: (i // steps_per_seq, 0, 0))
    return x_spec, mem_spec


def _in_proj_dn(x2, seq, norm_w, w_in, a_log, dt_bias, memq_w, mem_k, mem_v, *, tm=1024):
    m, d = x2.shape
    tm = min(tm, seq)
    heads = a_log.shape[0]
    mix = heads * HEAD_DIM
    n_mem, mw = mem_k.shape[1], mem_k.shape[2]
    w_all = w_in.astype(BF16)
    w_ab = _pad_cols(w_in[:, 4 * mix:4 * mix + 2 * heads]).astype(BF16)
    w_qm = w_in[:, 4 * mix + 2 * heads:].astype(BF16)
    x_spec, mem_spec = _in_proj_common_specs(tm, d, seq, n_mem, mw)
    row = lambda n: pl.BlockSpec((tm, n), lambda i: (i, 0))
    return pl.pallas_call(
        functools.partial(_in_proj_dn_kernel, n_chunk=512, heads=heads, chunk=DN_CHUNK),
        grid=(m // tm,),
        in_specs=[
            x_spec, _const_spec((1, d)), _const_spec(w_all.shape), _const_spec((d, LANES)),
            _const_spec((d, mw)), _const_spec((1, LANES)), _const_spec((1, LANES)),
            _const_spec((1, HEAD_DIM)), _const_spec((3 * LANES, 2 * LANES)), mem_spec, mem_spec,
        ],
        out_specs=[row(3 * mix), row(mix), row(3 * LANES), row(2 * LANES), row(mw)],
        out_shape=[
            jax.ShapeDtypeStruct((m, 3 * mix), BF16),
            jax.ShapeDtypeStruct((m, mix), BF16),
            jax.ShapeDtypeStruct((m, 3 * LANES), BF16),
            jax.ShapeDtypeStruct((m, 2 * LANES), BF16),
            jax.ShapeDtypeStruct((m, mw), BF16),
        ],
        compiler_params=pltpu.CompilerParams(
            dimension_semantics=("arbitrary",), vmem_limit_bytes=VMEM_LIMIT),
        name="in_proj_dn",
    )(x2, norm_w.reshape(1, d), w_all, w_ab, w_qm, _pad_lanes(a_log), _pad_lanes(dt_bias),
      memq_w.reshape(1, HEAD_DIM), _diff_selector(heads), mem_k, mem_v)


def _in_proj_fox(x2, seq, norm_w, w_in, f_bias, q_norm_w, k_norm_w, memq_w, mem_k, mem_v, *, tm=1024):
    m, d = x2.shape
    tm = min(tm, seq)
    heads = f_bias.shape[0]
    mix = heads * HEAD_DIM
    n_mem, mw = mem_k.shape[1], mem_k.shape[2]
    w_all = w_in.astype(BF16)
    w_f = _pad_cols(w_in[:, 4 * mix:4 * mix + heads]).astype(BF16)
    w_qm = w_in[:, 4 * mix + heads:].astype(BF16)
    x_spec, mem_spec = _in_proj_common_specs(tm, d, seq, n_mem, mw)
    row = lambda n: pl.BlockSpec((tm, n), lambda i: (i, 0))
    return pl.pallas_call(
        functools.partial(_in_proj_fox_kernel, n_chunk=512, steps_per_seq=seq // tm),
        grid=(m // tm,),
        in_specs=[
            x_spec, _const_spec((1, d)), _const_spec(w_all.shape), _const_spec((d, LANES)),
            _const_spec((d, mw)), _const_spec((1, LANES)), _const_spec((1, HEAD_DIM)),
            _const_spec((1, HEAD_DIM)), _const_spec((1, HEAD_DIM)),
            _const_spec((3 * LANES, 2 * LANES)), mem_spec, mem_spec,
        ],
        out_specs=[row(mix), row(mix), row(mix), row(mix), row(2 * LANES), row(mw)],
        out_shape=[jax.ShapeDtypeStruct((m, mix), BF16)] * 4 + [
            jax.ShapeDtypeStruct((m, 2 * LANES), BF16),
            jax.ShapeDtypeStruct((m, mw), BF16),
        ],
        scratch_shapes=[pltpu.VMEM((8, LANES), F32)],
        compiler_params=pltpu.CompilerParams(
            dimension_semantics=("arbitrary",), vmem_limit_bytes=VMEM_LIMIT),
        name="in_proj_fox",
    )(x2, norm_w.reshape(1, d), w_all, w_f, w_qm, _pad_lanes(f_bias),
      q_norm_w.reshape(1, HEAD_DIM), k_norm_w.reshape(1, HEAD_DIM),
      memq_w.reshape(1, HEAD_DIM), _diff_selector(heads), mem_k, mem_v)


def _dn_kernel(q_ref, k_ref, v_ref, z_ref, gb3_ref, gab_ref, cwq_ref, cwk_ref, cwv_ref, onw_ref,
               o_ref, e_s, ext_s, kq_s, nc_s, op_s, eg_s, state_s, *, hb, chunk, group, heads):
    seq = q_ref.shape[0]
    n_chunks = seq // chunk
    rows = group * chunk
    hg = pl.program_id(1)

    er = lax.broadcasted_iota(jnp.int32, (3 * LANES, 2 * LANES), 0)
    ec = lax.broadcasted_iota(jnp.int32, (3 * LANES, 2 * LANES), 1)
    for j in range(hb):
        head = hg * hb + j
        want = jnp.where(ec < LANES, head, heads + head)
        e_s[j] = jnp.where(jnp.bitwise_and(er, LANES - 1) == want, 1.0, 0.0).astype(BF16)
        state_s[j] = jnp.zeros((HEAD_DIM, HEAD_DIM), F32)

    def conv_silu(ref, cw_ref, ext_ref, sl, r0, first):
        main = ref[pl.ds(r0, rows), sl].astype(F32)
        h0 = pl.multiple_of(jnp.maximum(r0 - BF16_ROWS, 0), BF16_ROWS)
        ext_ref[0:BF16_ROWS, :] = jnp.where(first, 0.0, ref[pl.ds(h0, BF16_ROWS), sl].astype(F32))
        ext_ref[BF16_ROWS:, :] = main
        cw = cw_ref[:, sl]
        acc = main * cw[CONV_WIDTH - 1:CONV_WIDTH, :]
        for s in range(1, CONV_WIDTH):
            acc = acc + ext_ref[BF16_ROWS - s:BF16_ROWS - s + rows, :] * cw[
                CONV_WIDTH - 1 - s:CONV_WIDTH - s, :]
        return acc * _sigmoid(acc)

    def l2n(x):
        return x * lax.rsqrt(jnp.sum(x * x, axis=-1, keepdims=True) + EPS)

    assert 2 * chunk == LANES
    ci = lax.broadcasted_iota(jnp.int32, (group, chunk, LANES), 1)
    cj = lax.broadcasted_iota(jnp.int32, (group, chunk, LANES), 2)
    causal = ci >= cj
    strict = ci > cj
    eye_right = jnp.where(cj - chunk == ci, 1.0, 0.0)
    lane2 = lax.broadcasted_iota(jnp.int32, (rows, LANES), 1)
    zeros_rows = jnp.zeros((group, chunk, LANES), BF16)
    nb = hb * group
    left = lax.broadcasted_iota(jnp.int32, (nb, chunk, LANES), 2) < chunk
    zeros_rows_all = jnp.zeros((nb, chunk, LANES), BF16)
    zeros_wide_all = jnp.zeros((nb, chunk, 2 * HEAD_DIM), BF16)
    n_double = chunk.bit_length() - 1
    shape3 = (group, chunk, HEAD_DIM)

    def bmm(a, b):
        return jnp.einsum("gij,gjk->gik", a, b, preferred_element_type=F32)

    def bmm_nt(a, b):
        return jnp.einsum("gid,gjd->gij", a, b, preferred_element_type=F32)

    def prep(gi, carry):
        r0 = pl.multiple_of(gi * rows, rows)
        c0 = pl.multiple_of(gi * group, group)
        first = gi == 0

        def front(j):
            sl = slice(j * HEAD_DIM, (j + 1) * HEAD_DIM)
            q = (l2n(conv_silu(q_ref, cwq_ref, ext_s.at[0], sl, r0, first))
                 * (HEAD_DIM ** -0.5)).reshape(shape3)
            k = l2n(conv_silu(k_ref, cwk_ref, ext_s.at[1], sl, r0, first)).reshape(shape3)
            v = conv_silu(v_ref, cwv_ref, ext_s.at[2], sl, r0, first).reshape(shape3)
            gbb = _dot(gb3_ref[pl.ds(r0, rows), :], e_s[j])
            g = gbb[:, :LANES].reshape(shape3)
            beta = gbb[:, LANES:].reshape(shape3)
            gab = gab_ref[pl.ds(r0, rows), :]
            own = jnp.right_shift(lane2, 3) == hg * hb + j
            ga = gab[:, :LANES].reshape(shape3)
            gbm = jnp.where(own, gab[:, LANES:], jnp.zeros_like(gab[:, LANES:])).reshape(shape3)
            diff = bmm_nt(ga, jnp.concatenate([gbm, zeros_rows], axis=1))
            decay = jnp.where(causal, jnp.exp(jnp.where(causal, diff, 0.0)), 0.0)
            kb = k * beta
            kq = jnp.concatenate([kb, q], axis=1).astype(BF16)
            aq = bmm_nt(kq, jnp.concatenate([k.astype(BF16), zeros_rows], axis=1))
            qk = (aq[:, chunk:] * decay)[:, :, :chunk].astype(BF16)
            y = eye_right - jnp.where(strict, aq[:, :chunk] * decay, 0.0)
            return q, k, v, g, beta, kb, qk, y

        def back(per_head):
            q, k, v, g, beta, kb, qk, y = [jnp.concatenate(t, axis=0) for t in zip(*per_head)]
            for _ in range(n_double):
                yb = y.astype(BF16)
                py = bmm(yb, jnp.concatenate([yb, zeros_rows_all], axis=1))
                y = jnp.where(left, py, y + py)
            eg = jnp.exp(g)
            rhs = jnp.concatenate([v * beta, kb * eg], axis=2).astype(BF16)
            x = bmm(y.astype(BF16), jnp.concatenate([zeros_wide_all, rhs], axis=1))
            xb = x.astype(BF16)
            qx = bmm(qk, xb)
            g_last = g[:, chunk - 1:chunk, :]
            kd = (k * jnp.exp(g_last - g)).astype(BF16)
            qp = (q * eg - qx[:, :, HEAD_DIM:]).astype(BF16)
            eg_last = jnp.broadcast_to(jnp.exp(g_last), (nb, 8, LANES))
            for j in range(hb):
                for c in range(group):
                    kx = _dot_tn(kd[j * group + c], xb[j * group + c])
                    nc_s[j, c0 + c] = kx[:, :HEAD_DIM]
                    kq_s[j, c0 + c, :HEAD_DIM, :] = kx[:, HEAD_DIM:].astype(BF16)
                mine = slice(j * group, (j + 1) * group)
                kq_s[j, pl.ds(c0, group), HEAD_DIM:, :] = qp[mine]
                op_s[j, pl.ds(c0, group)] = qx[mine, :, :HEAD_DIM]
                eg_s[j, pl.ds(c0, group)] = eg_last[mine]

        back([front(j) for j in range(hb)])
        return carry

    lax.fori_loop(0, n_chunks // group, prep, 0)

    def step(c, carry):
        r0 = pl.multiple_of(c * chunk, chunk)
        for j in range(hb):
            sl = slice(j * HEAD_DIM, (j + 1) * HEAD_DIM)
            state = state_s[j]
            sb = state.astype(BF16)
            ks = _dot(kq_s[j, c], sb)
            out = ks[HEAD_DIM:] + op_s[j, c]
            state_s[j] = state * eg_s[j, c][0:1, :] + nc_s[j, c] - ks[:HEAD_DIM]
            silu_z = z_ref[pl.ds(r0, chunk), sl].astype(F32)
            o_ref[pl.ds(r0, chunk), sl] = (_rms(out, onw_ref[...]) * silu_z).astype(BF16)
        return carry

    lax.fori_loop(0, n_chunks, step, 0, unroll=STEP_UNROLL)


def _deltanet(qkv, z, gb3, gab, conv_w, o_norm_w, batch, seq, heads, *, hb=4, group=8):
    m = qkv.shape[0]
    mix = heads * HEAD_DIM
    chunk = DN_CHUNK
    group = min(group, seq // chunk)
    n_chunks = seq // chunk
    hgroups = heads // hb
    wblk = hb * HEAD_DIM
    col = lambda off: pl.BlockSpec((seq, wblk), lambda b, h: (b, off + h))
    cw = lambda off: pl.BlockSpec((CONV_WIDTH, wblk), lambda b, h: (0, off + h))
    return pl.pallas_call(
        functools.partial(_dn_kernel, hb=hb, chunk=chunk, group=group, heads=heads),
        grid=(batch, hgroups),
        in_specs=[
            col(0), col(hgroups), col(2 * hgroups),
            pl.BlockSpec((seq, wblk), lambda b, h: (b, h)),
            pl.BlockSpec((seq, 3 * LANES), lambda b, h: (b, 0)),
            pl.BlockSpec((seq, 2 * LANES), lambda b, h: (b, 0)),
            cw(0), cw(hgroups), cw(2 * hgroups),
            pl.BlockSpec((1, HEAD_DIM), lambda b, h: (0, 0)),
        ],
        out_specs=pl.BlockSpec((seq, wblk), lambda b, h: (b, h)),
        out_shape=jax.ShapeDtypeStruct((m, mix), BF16),
        scratch_shapes=[
            pltpu.VMEM((hb, 3 * LANES, 2 * LANES), BF16),
            pltpu.VMEM((3, group * chunk + BF16_ROWS, HEAD_DIM), F32),
            pltpu.VMEM((hb, n_chunks, HEAD_DIM + chunk, HEAD_DIM), BF16),
            pltpu.VMEM((hb, n_chunks, HEAD_DIM, HEAD_DIM), F32),
            pltpu.VMEM((hb, n_chunks, chunk, HEAD_DIM), F32),
            pltpu.VMEM((hb, n_chunks, 8, LANES), F32),
            pltpu.VMEM((hb, HEAD_DIM, HEAD_DIM), F32),
        ],
        compiler_params=pltpu.CompilerParams(
            dimension_semantics=("arbitrary", "arbitrary"), vmem_limit_bytes=VMEM_LIMIT),
        name="deltanet",
    )(qkv, qkv, qkv, z, gb3, gab, conv_w, conv_w, conv_w, o_norm_w.reshape(1, HEAD_DIM))


def _fox_kernel(q_ref, k_ref, v_ref, gate_ref, fab_ref, o_ref, qa_s, ka_s, vt_s, *, tq, hb):
    seq = q_ref.shape[0]
    n_q = seq // tq
    lane = lax.broadcasted_iota(jnp.int32, (seq, LANES), 1)
    key_le_query = (lax.broadcasted_iota(jnp.int32, (tq, tq), 0)
                    <= lax.broadcasted_iota(jnp.int32, (tq, tq), 1))

    def col_max(m, s):
        sm = jnp.max(s, axis=0, keepdims=True)
        return sm if m is None else jnp.maximum(m, sm)

    def head_program(j):
        sl = slice(j * HEAD_DIM, (j + 1) * HEAD_DIM)
        head = pl.program_id(1) * hb + j
        fb = fab_ref[:, LANES:]
        qa_s[j, :, :HEAD_DIM] = q_ref[:, sl]
        qa_s[j, :, HEAD_DIM:] = fab_ref[:, :LANES]
        ka_s[j, :, :HEAD_DIM] = k_ref[:, sl]
        ka_s[j, :, HEAD_DIM:] = jnp.where(jnp.right_shift(lane, 3) == head, fb, jnp.zeros_like(fb))
        vt_s[j, :HEAD_DIM, :] = v_ref[:, sl].astype(F32).T.astype(BF16)
        vt_s[j, HEAD_DIM:, :] = jnp.where(
            lax.broadcasted_iota(jnp.int32, (BF16_ROWS, seq), 0) == 0, 1.0, 0.0).astype(BF16)

        def score_tile(qi, kb):
            s = _dot_nt(ka_s[j, kb * tq:(kb + 1) * tq, :], qa_s[j, qi * tq:(qi + 1) * tq, :])
            return jnp.where(key_le_query, s, -jnp.inf) if kb == qi else s

        tiles = [score_tile(0, 0)]
        m = col_max(None, tiles[0])
        yield
        for qi in range(n_q):
            lo, hi = qi * tq, (qi + 1) * tq
            nxt_tiles, nxt_m, p_tiles = [], None, []
            n_next = qi + 2 if qi + 1 < n_q else 0
            for t in range(max(n_next, qi + 1)):
                if t < n_next:
                    nxt_tiles.append(score_tile(qi + 1, t))
                    nxt_m = col_max(nxt_m, nxt_tiles[-1])
                if t <= qi:
                    p_tiles.append(jnp.exp2(tiles[t] - m).astype(BF16))
            pt = p_tiles[0] if qi == 0 else jnp.concatenate(p_tiles, axis=0)
            ot = _dot(vt_s[j, :, :hi], pt)
            gate = gate_ref[lo:hi, sl].astype(F32)
            o = ot[:HEAD_DIM] / ot[HEAD_DIM:HEAD_DIM + 1]
            o_ref[lo:hi, sl] = (o.T * gate).astype(BF16)
            tiles, m = nxt_tiles, nxt_m
            yield

    programs = [head_program(j) for j in range(hb)]
    for _ in range(n_q + 1):
        for prog in programs:
            next(prog)


def _fox_attention(q, k, v, gate, fab, batch, seq, heads, *, tq=256, hb=4):
    m, mix = q.shape
    tq = min(tq, seq)
    col = pl.BlockSpec((seq, hb * HEAD_DIM), lambda b, h: (b, h))
    return pl.pallas_call(
        functools.partial(_fox_kernel, tq=tq, hb=hb),
        grid=(batch, heads // hb),
        in_specs=[col, col, col, col, pl.BlockSpec((seq, 2 * LANES), lambda b, h: (b, 0))],
        out_specs=col,
        out_shape=jax.ShapeDtypeStruct((m, mix), BF16),
        scratch_shapes=[
            pltpu.VMEM((hb, seq, 2 * HEAD_DIM), BF16),
            pltpu.VMEM((hb, seq, 2 * HEAD_DIM), BF16),
            pltpu.VMEM((hb, HEAD_DIM + BF16_ROWS, seq), BF16),
        ],
        compiler_params=pltpu.CompilerParams(
            dimension_semantics=("arbitrary", "arbitrary"), vmem_limit_bytes=VMEM_LIMIT),
        name="fox_attention",
    )(q, k, v, gate, fab)


def _post_kernel(x_ref, mix_ref, mo_ref, wo_ref, nw_ref, w1_ref, w2_ref, y_ref, hid_s, *, f_chunk):
    mixw = mix_ref.shape[1]
    x1 = x_ref[...] + _dot(mix_ref[...], wo_ref[:mixw, :]) + _dot(mo_ref[...], wo_ref[mixw:, :])
    h = _rms(x1, nw_ref[...]).astype(BF16)
    for f0 in range(0, w1_ref.shape[1], f_chunk):
        t = jnp.maximum(_dot(h, w1_ref[:, f0:f0 + f_chunk]), 0.0)
        hid_s[:, f0:f0 + f_chunk] = (t * t).astype(BF16)
    y_ref[...] = x1 + _dot(hid_s[...], w2_ref[...])


def _post(x2, mix, mem_out, w_out, norm_w, w1, w2, layer, *, tm=512, f_chunk=512):
    m, d = x2.shape
    mixw, mw, dff = mix.shape[1], mem_out.shape[1], w1.shape[2]
    row = lambda n: pl.BlockSpec((tm, n), lambda i: (i, 0))
    layer_spec = lambda r, c: pl.BlockSpec(
        (None, r, c), lambda i: (layer, 0, 0), pipeline_mode=pl.Buffered(1))
    return pl.pallas_call(
        functools.partial(_post_kernel, f_chunk=f_chunk),
        grid=(m // tm,),
        in_specs=[
            row(d), row(mixw), row(mw), layer_spec(mixw + mw, d), _const_spec((1, d)),
            layer_spec(d, dff), layer_spec(dff, d),
        ],
        out_specs=row(d),
        out_shape=jax.ShapeDtypeStruct((m, d), F32),
        scratch_shapes=[pltpu.VMEM((tm, dff), BF16)],
        compiler_params=pltpu.CompilerParams(
            dimension_semantics=("arbitrary",), vmem_limit_bytes=VMEM_LIMIT),
        name="post",
    )(x2, mix, mem_out, w_out, norm_w.reshape(1, d), w1, w2)


def kernel(x, mem, mem_norm_w, w_mem_kv, mem_k_norm_w, norm1_w, dn_w_in, dn_conv_w, dn_a_log,
           dn_dt_bias, dn_o_norm_w, fox_w_in, fox_f_bias, fox_q_norm_w, fox_k_norm_w, memq_norm_w,
           w_out, norm2_w, w_mlp1, w_mlp2):
    batch, seq, d = x.shape
    depth = norm1_w.shape[0]
    mem_k, mem_v = _mem_kv(mem, mem_norm_w, w_mem_kv, mem_k_norm_w)
    x2 = x.reshape(batch * seq, d)
    w_out_b, w_mlp1_b, w_mlp2_b = w_out.astype(BF16), w_mlp1.astype(BF16), w_mlp2.astype(BF16)
    for i in range(depth):
        j = i // 2
        if i % 2 == 0:
            heads = dn_a_log.shape[1]
            qkv, z, gb3, gab, mem_out = _in_proj_dn(
                x2, seq, norm1_w[i], dn_w_in[j], dn_a_log[j], dn_dt_bias[j], memq_norm_w[i],
                mem_k, mem_v)
            mix = _deltanet(qkv, z, gb3, gab, dn_conv_w[j], dn_o_norm_w[j], batch, seq, heads)
        else:
            heads = fox_f_bias.shape[1]
            q, k, v, gate, fab, mem_out = _in_proj_fox(
                x2, seq, norm1_w[i], fox_w_in[j], fox_f_bias[j], fox_q_norm_w[j],
                fox_k_norm_w[j], memq_norm_w[i], mem_k, mem_v)
            mix = _fox_attention(q, k, v, gate, fab, batch, seq, heads)
        x2 = _post(x2, mix, mem_out, w_out_b, norm2_w[i], w_mlp1_b, w_mlp2_b, i)
    return x2.reshape(batch, seq, d)
```

```python
import functools

import jax
import jax.numpy as jnp
from jax import lax
from jax.experimental import pallas as pl
from jax.experimental.pallas import tpu as pltpu

F32 = jnp.float32
BF16 = jnp.bfloat16

HEAD_DIM = 128
LANES = 128
BF16_ROWS = 16
MEM_HEADS = 4
CONV_WIDTH = 4
EPS = 1e-6
DN_CHUNK = 64
STEP_UNROLL = 16
VMEM_LIMIT = 56 * 1024 * 1024
LOG2E = 1.4426950408889634
ONES_LANE = LANES - 1


def _sigmoid(x):
    return 1.0 / (1.0 + jnp.exp(-x))


def _softplus(x):
    return jnp.maximum(x, 0.0) + jnp.log1p(jnp.exp(-jnp.abs(x)))


def _rms(x, w):
    return x * lax.rsqrt(jnp.mean(x * x, axis=-1, keepdims=True) + EPS) * w


def _dot(a, b):
    return jnp.dot(a, b, preferred_element_type=F32)


def _dot_nt(a, b):
    return lax.dot_general(a, b, (((1,), (1,)), ((), ())), preferred_element_type=F32)


def _dot_tn(a, b):
    return lax.dot_general(a, b, (((0,), (0,)), ((), ())), preferred_element_type=F32)


def _split3(x):
    hi = x.astype(BF16).astype(F32)
    r = x - hi
    mid = r.astype(BF16).astype(F32)
    lo = (r - mid).astype(BF16).astype(F32)
    return hi, mid, lo


DIFF_LANES = 8


def _diff_selector(heads):
    assert heads * DIFF_LANES <= LANES and heads <= ONES_LANE
    s = [[0.0] * (2 * LANES) for _ in range(3 * LANES)]
    for h in range(heads):
        for part in range(3):
            s[part * LANES + h][DIFF_LANES * h + part] = 1.0
            s[ONES_LANE][DIFF_LANES * h + 3 + part] = 1.0
            s[ONES_LANE][LANES + DIFF_LANES * h + part] = 1.0
            s[part * LANES + h][LANES + DIFF_LANES * h + 3 + part] = -1.0
    return jnp.array(s, BF16)


def _slabs_with_one(x, lane):
    hi, mid, lo = _split3(x)
    spare = lane == ONES_LANE
    return jnp.concatenate([jnp.where(spare, 1.0, hi), jnp.where(spare, 0.0, mid),
                            jnp.where(spare, 0.0, lo)], axis=1).astype(BF16)


def _chunk_cumsum(x, row_in_chunk, length):
    s = 1
    while s < length:
        x = x + jnp.where(row_in_chunk >= s, pltpu.roll(x, s, 0), 0.0)
        s *= 2
    return x


def _const_spec(shape):
    return pl.BlockSpec(shape, lambda *_: (0,) * len(shape), pipeline_mode=pl.Buffered(1))


def _mem_kv_kernel(mem_ref, nw_ref, w_ref, knw_ref, k_ref, v_ref):
    h = _rms(mem_ref[0], nw_ref[...]).astype(BF16)
    kv = _dot(h, w_ref[...])
    mw = kv.shape[1] // 2
    for hd in range(mw // HEAD_DIM):
        sl = slice(hd * HEAD_DIM, (hd + 1) * HEAD_DIM)
        k_ref[0, :, sl] = _rms(kv[:, sl], knw_ref[...]).astype(BF16)
    v_ref[0] = kv[:, mw:].astype(BF16)


def _mem_kv(mem, mem_norm_w, w_mem_kv, mem_k_norm_w):
    b, n_mem, d = mem.shape
    mw = w_mem_kv.shape[1] // 2
    return pl.pallas_call(
        _mem_kv_kernel,
        grid=(b,),
        in_specs=[
            pl.BlockSpec((1, n_mem, d), lambda i: (i, 0, 0)),
            _const_spec((1, d)),
            _const_spec((d, 2 * mw)),
            _const_spec((1, HEAD_DIM)),
        ],
        out_specs=[
            pl.BlockSpec((1, n_mem, mw), lambda i: (i, 0, 0)),
            pl.BlockSpec((1, n_mem, mw), lambda i: (i, 0, 0)),
        ],
        out_shape=[jax.ShapeDtypeStruct((b, n_mem, mw), BF16)] * 2,
        compiler_params=pltpu.CompilerParams(
            dimension_semantics=("arbitrary",), vmem_limit_bytes=VMEM_LIMIT),
        name="mem_kv",
    )(mem, mem_norm_w.reshape(1, d), w_mem_kv.astype(BF16), mem_k_norm_w.reshape(1, HEAD_DIM))


def _mem_attention(qm, mqw, mk_ref, mv_ref, mo_ref):
    scale = HEAD_DIM ** -0.5
    for hd in range(MEM_HEADS):
        sl = slice(hd * HEAD_DIM, (hd + 1) * HEAD_DIM)
        q = (_rms(qm[:, sl], mqw) * scale).astype(BF16)
        s = _dot_nt(q, mk_ref[0, :, sl])
        p = jnp.exp(s - jnp.max(s, axis=1, keepdims=True))
        o = _dot(p.astype(BF16), mv_ref[0, :, sl]) / jnp.sum(p, axis=1, keepdims=True)
        mo_ref[:, sl] = o.astype(BF16)


def _in_proj_dn_kernel(x_ref, nw_ref, wmain_ref, wab_ref, wqm_ref, alog_ref, dtb_ref, mqw_ref,
                       sel_ref, mk_ref, mv_ref, qkv_ref, z_ref, gb3_ref, gab_ref, mo_ref,
                       *, n_chunk, heads, chunk):
    h = _rms(x_ref[...], nw_ref[...]).astype(BF16)
    _mem_attention(_dot(h, wqm_ref[...]), mqw_ref[...], mk_ref, mv_ref, mo_ref)
    ab = _dot(h, wab_ref[...])
    lane = lax.broadcasted_iota(jnp.int32, ab.shape, 1)
    row = lax.broadcasted_iota(jnp.int32, ab.shape, 0)
    g = -jnp.exp(alog_ref[...]) * _softplus(ab + dtb_ref[...])
    g_cum = _chunk_cumsum(g, jnp.bitwise_and(row, chunk - 1), chunk)
    slabs = _slabs_with_one(jnp.where(lane < heads, g_cum, _sigmoid(ab)), lane)
    gb3_ref[...] = slabs
    gab_ref[...] = _dot(slabs, sel_ref[...]).astype(BF16)
    qkv_w = qkv_ref.shape[1]
    for n0 in range(0, qkv_w, n_chunk):
        qkv_ref[:, n0:n0 + n_chunk] = _dot(h, wmain_ref[:, n0:n0 + n_chunk]).astype(BF16)
    for n0 in range(0, z_ref.shape[1], n_chunk):
        zc = _dot(h, wmain_ref[:, qkv_w + n0:qkv_w + n0 + n_chunk])
        z_ref[:, n0:n0 + n_chunk] = (zc * _sigmoid(zc)).astype(BF16)


def _in_proj_fox_kernel(x_ref, nw_ref, wmain_ref, wf_ref, wqm_ref, fb_ref, qnw_ref, knw_ref,
                        mqw_ref, sel_ref, mk_ref, mv_ref, q_ref, k_ref, v_ref, gate_ref, fab_ref,
                        mo_ref, carry_s, *, n_chunk, steps_per_seq):
    h = _rms(x_ref[...], nw_ref[...]).astype(BF16)
    _mem_attention(_dot(h, wqm_ref[...]), mqw_ref[...], mk_ref, mv_ref, mo_ref)

    @pl.when(pl.program_id(0) % steps_per_seq == 0)
    def _():
        carry_s[...] = jnp.zeros_like(carry_s)

    lf = -_softplus(-(_dot(h, wf_ref[...]) + fb_ref[...])) * LOG2E
    tm = lf.shape[0]
    row = lax.broadcasted_iota(jnp.int32, lf.shape, 0)
    lane = lax.broadcasted_iota(jnp.int32, lf.shape, 1)
    f_cum = _chunk_cumsum(lf, row, tm) + carry_s[0:1, :]
    carry_s[...] = jnp.broadcast_to(f_cum[tm - 1:tm, :], carry_s.shape)
    fab_ref[...] = _dot(_slabs_with_one(f_cum, lane), sel_ref[...]).astype(BF16)

    width = q_ref.shape[1]
    scale = HEAD_DIM ** -0.5 * LOG2E
    for n0 in range(0, width, n_chunk):
        yq = _dot(h, wmain_ref[:, n0:n0 + n_chunk])
        yk = _dot(h, wmain_ref[:, width + n0:width + n0 + n_chunk])
        for c0 in range(0, n_chunk, HEAD_DIM):
            q_ref[:, n0 + c0:n0 + c0 + HEAD_DIM] = (
                _rms(yq[:, c0:c0 + HEAD_DIM], qnw_ref[...]) * scale).astype(BF16)
            k_ref[:, n0 + c0:n0 + c0 + HEAD_DIM] = _rms(
                yk[:, c0:c0 + HEAD_DIM], knw_ref[...]).astype(BF16)
        v_ref[:, n0:n0 + n_chunk] = _dot(
            h, wmain_ref[:, 2 * width + n0:2 * width + n0 + n_chunk]).astype(BF16)
        gate_ref[:, n0:n0 + n_chunk] = _sigmoid(_dot(
            h, wmain_ref[:, 3 * width + n0:3 * width + n0 + n_chunk])).astype(BF16)


def _pad_lanes(v):
    return jnp.zeros((1, LANES), F32).at[0, :v.shape[0]].set(v.astype(F32))


def _pad_cols(w):
    return jnp.zeros((w.shape[0], LANES), w.dtype).at[:, :w.shape[1]].set(w)


def _in_proj_common_specs(tm, d, seq, n_mem, mw):
    assert seq % tm == 0, "a row block must not straddle two sequences"
    steps_per_seq = seq // tm
    x_spec = pl.BlockSpec((tm, d), lambda i: (i, 0))
    mem_spec = pl.BlockSpec((1, n_mem, mw), lambda i: (i // steps_per_seq, 0, 0))
    return x_spec, mem_spec


def _in_proj_dn(x2, seq, norm_w, w_in, a_log, dt_bias, memq_w, mem_k, mem_v, *, tm=1024):
    m, d = x2.shape
    tm = min(tm, seq)
    heads = a_log.shape[0]
    mix = heads * HEAD_DIM
    n_mem, mw = mem_k.shape[1], mem_k.shape[2]
    w_all = w_in.astype(BF16)
    w_ab = _pad_cols(w_in[:, 4 * mix:4 * mix + 2 * heads]).astype(BF16)
    w_qm = w_in[:, 4 * mix + 2 * heads:].astype(BF16)
    x_spec, mem_spec = _in_proj_common_specs(tm, d, seq, n_mem, mw)
    row = lambda n: pl.BlockSpec((tm, n), lambda i: (i, 0))
    return pl.pallas_call(
        functools.partial(_in_proj_dn_kernel, n_chunk=512, heads=heads, chunk=DN_CHUNK),
        grid=(m // tm,),
        in_specs=[
            x_spec, _const_spec((1, d)), _const_spec(w_all.shape), _const_spec((d, LANES)),
            _const_spec((d, mw)), _const_spec((1, LANES)), _const_spec((1, LANES)),
            _const_spec((1, HEAD_DIM)), _const_spec((3 * LANES, 2 * LANES)), mem_spec, mem_spec,
        ],
        out_specs=[row(3 * mix), row(mix), row(3 * LANES), row(2 * LANES), row(mw)],
        out_shape=[
            jax.ShapeDtypeStruct((m, 3 * mix), BF16),
            jax.ShapeDtypeStruct((m, mix), BF16),
            jax.ShapeDtypeStruct((m, 3 * LANES), BF16),
            jax.ShapeDtypeStruct((m, 2 * LANES), BF16),
            jax.ShapeDtypeStruct((m, mw), BF16),
        ],
        compiler_params=pltpu.CompilerParams(
            dimension_semantics=("arbitrary",), vmem_limit_bytes=VMEM_LIMIT),
        name="in_proj_dn",
    )(x2, norm_w.reshape(1, d), w_all, w_ab, w_qm, _pad_lanes(a_log), _pad_lanes(dt_bias),
      memq_w.reshape(1, HEAD_DIM), _diff_selector(heads), mem_k, mem_v)


def _in_proj_fox(x2, seq, norm_w, w_in, f_bias, q_norm_w, k_norm_w, memq_w, mem_k, mem_v, *, tm=1024):
    m, d = x2.shape
    tm = min(tm, seq)
    heads = f_bias.shape[0]
    mix = heads * HEAD_DIM
    n_mem, mw = mem_k.shape[1], mem_k.shape[2]
    w_all = w_in.astype(BF16)
    w_f = _pad_cols(w_in[:, 4 * mix:4 * mix + heads]).astype(BF16)
    w_qm = w_in[:, 4 * mix + heads:].astype(BF16)
    x_spec, mem_spec = _in_proj_common_specs(tm, d, seq, n_mem, mw)
    row = lambda n: pl.BlockSpec((tm, n), lambda i: (i, 0))
    return pl.pallas_call(
        functools.partial(_in_proj_fox_kernel, n_chunk=512, steps_per_seq=seq // tm),
        grid=(m // tm,),
        in_specs=[
            x_spec, _const_spec((1, d)), _const_spec(w_all.shape), _const_spec((d, LANES)),
            _const_spec((d, mw)), _const_spec((1, LANES)), _const_spec((1, HEAD_DIM)),
            _const_spec((1, HEAD_DIM)), _const_spec((1, HEAD_DIM)),
            _const_spec((3 * LANES, 2 * LANES)), mem_spec, mem_spec,
        ],
        out_specs=[row(mix), row(mix), row(mix), row(mix), row(2 * LANES), row(mw)],
        out_shape=[jax.ShapeDtypeStruct((m, mix), BF16)] * 4 + [
            jax.ShapeDtypeStruct((m, 2 * LANES), BF16),
            jax.ShapeDtypeStruct((m, mw), BF16),
        ],
        scratch_shapes=[pltpu.VMEM((8, LANES), F32)],
        compiler_params=pltpu.CompilerParams(
            dimension_semantics=("arbitrary",), vmem_limit_bytes=VMEM_LIMIT),
        name="in_proj_fox",
    )(x2, norm_w.reshape(1, d), w_all, w_f, w_qm, _pad_lanes(f_bias),
      q_norm_w.reshape(1, HEAD_DIM), k_norm_w.reshape(1, HEAD_DIM),
      memq_w.reshape(1, HEAD_DIM), _diff_selector(heads), mem_k, mem_v)


def _dn_kernel(q_ref, k_ref, v_ref, z_ref, gb3_ref, gab_ref, cwq_ref, cwk_ref, cwv_ref, onw_ref,
               o_ref, e_s, ext_s, kq_s, nc_s, op_s, eg_s, state_s, *, hb, chunk, group, heads):
    seq = q_ref.shape[0]
    n_chunks = seq // chunk
    rows = group * chunk
    hg = pl.program_id(1)

    er = lax.broadcasted_iota(jnp.int32, (3 * LANES, 2 * LANES), 0)
    ec = lax.broadcasted_iota(jnp.int32, (3 * LANES, 2 * LANES), 1)
    for j in range(hb):
        head = hg * hb + j
        want = jnp.where(ec < LANES, head, heads + head)
        e_s[j] = jnp.where(jnp.bitwise_and(er, LANES - 1) == want, 1.0, 0.0).astype(BF16)
        state_s[j] = jnp.zeros((HEAD_DIM, HEAD_DIM), F32)

    def conv_silu(ref, cw_ref, ext_ref, sl, r0, first):
        main = ref[pl.ds(r0, rows), sl].astype(F32)
        h0 = pl.multiple_of(jnp.maximum(r0 - BF16_ROWS, 0), BF16_ROWS)
        ext_ref[0:BF16_ROWS, :] = jnp.where(first, 0.0, ref[pl.ds(h0, BF16_ROWS), sl].astype(F32))
        ext_ref[BF16_ROWS:, :] = main
        cw = cw_ref[:, sl]
        acc = main * cw[CONV_WIDTH - 1:CONV_WIDTH, :]
        for s in range(1, CONV_WIDTH):
            acc = acc + ext_ref[BF16_ROWS - s:BF16_ROWS - s + rows, :] * cw[
                CONV_WIDTH - 1 - s:CONV_WIDTH - s, :]
        return acc * _sigmoid(acc)

    def l2n(x):
        return x * lax.rsqrt(jnp.sum(x * x, axis=-1, keepdims=True) + EPS)

    assert 2 * chunk == LANES
    ci = lax.broadcasted_iota(jnp.int32, (group, chunk, LANES), 1)
    cj = lax.broadcasted_iota(jnp.int32, (group, chunk, LANES), 2)
    causal = ci >= cj
    strict = ci > cj
    eye_right = jnp.where(cj - chunk == ci, 1.0, 0.0)
    lane2 = lax.broadcasted_iota(jnp.int32, (rows, LANES), 1)
    zeros_rows = jnp.zeros((group, chunk, LANES), BF16)
    nb = hb * group
    left = lax.broadcasted_iota(jnp.int32, (nb, chunk, LANES), 2) < chunk
    zeros_rows_all = jnp.zeros((nb, chunk, LANES), BF16)
    zeros_wide_all = jnp.zeros((nb, chunk, 2 * HEAD_DIM), BF16)
    n_double = chunk.bit_length() - 1
    shape3 = (group, chunk, HEAD_DIM)

    def bmm(a, b):
        return jnp.einsum("gij,gjk->gik", a, b, preferred_element_type=F32)

    def bmm_nt(a, b):
        return jnp.einsum("gid,gjd->gij", a, b, preferred_element_type=F32)

    def prep(gi, carry):
        r0 = pl.multiple_of(gi * rows, rows)
        c0 = pl.multiple_of(gi * group, group)
        first = gi == 0

        def front(j):
            sl = slice(j * HEAD_DIM, (j + 1) * HEAD_DIM)
            q = (l2n(conv_silu(q_ref, cwq_ref, ext_s.at[0], sl, r0, first))
                 * (HEAD_DIM ** -0.5)).reshape(shape3)
            k = l2n(conv_silu(k_ref, cwk_ref, ext_s.at[1], sl, r0, first)).reshape(shape3)
            v = conv_silu(v_ref, cwv_ref, ext_s.at[2], sl, r0, first).reshape(shape3)
            gbb = _dot(gb3_ref[pl.ds(r0, rows), :], e_s[j])
            g = gbb[:, :LANES].reshape(shape3)
            beta = gbb[:, LANES:].reshape(shape3)
            gab = gab_ref[pl.ds(r0, rows), :]
            own = jnp.right_shift(lane2, 3) == hg * hb + j
            ga = gab[:, :LANES].reshape(shape3)
            gbm = jnp.where(own, gab[:, LANES:], jnp.zeros_like(gab[:, LANES:])).reshape(shape3)
            diff = bmm_nt(ga, jnp.concatenate([gbm, zeros_rows], axis=1))
            decay = jnp.where(causal, jnp.exp(jnp.where(causal, diff, 0.0)), 0.0)
            kb = k * beta
            kq = jnp.concatenate([kb, q], axis=1).astype(BF16)
            aq = bmm_nt(kq, jnp.concatenate([k.astype(BF16), zeros_rows], axis=1))
            qk = (aq[:, chunk:] * decay)[:, :, :chunk].astype(BF16)
            y = eye_right - jnp.where(strict, aq[:, :chunk] * decay, 0.0)
            return q, k, v, g, beta, kb, qk, y

        def back(per_head):
            q, k, v, g, beta, kb, qk, y = [jnp.concatenate(t, axis=0) for t in zip(*per_head)]
            for _ in range(n_double):
                yb = y.astype(BF16)
                py = bmm(yb, jnp.concatenate([yb, zeros_rows_all], axis=1))
                y = jnp.where(left, py, y + py)
            eg = jnp.exp(g)
            rhs = jnp.concatenate([v * beta, kb * eg], axis=2).astype(BF16)
            x = bmm(y.astype(BF16), jnp.concatenate([zeros_wide_all, rhs], axis=1))
            xb = x.astype(BF16)
            qx = bmm(qk, xb)
            g_last = g[:, chunk - 1:chunk, :]
            kd = (k * jnp.exp(g_last - g)).astype(BF16)
            qp = (q * eg - qx[:, :, HEAD_DIM:]).astype(BF16)
            eg_last = jnp.broadcast_to(jnp.exp(g_last), (nb, 8, LANES))
            for j in range(hb):
                for c in range(group):
                    kx = _dot_tn(kd[j * group + c], xb[j * group + c])
                    nc_s[j, c0 + c] = kx[:, :HEAD_DIM]
                    kq_s[j, c0 + c, :HEAD_DIM, :] = kx[:, HEAD_DIM:].astype(BF16)
                mine = slice(j * group, (j + 1) * group)
                kq_s[j, pl.ds(c0, group), HEAD_DIM:, :] = qp[mine]
                op_s[j, pl.ds(c0, group)] = qx[mine, :, :HEAD_DIM]
                eg_s[j, pl.ds(c0, group)] = eg_last[mine]

        back([front(j) for j in range(hb)])
        return carry

    lax.fori_loop(0, n_chunks // group, prep, 0)

    def step(c, carry):
        r0 = pl.multiple_of(c * chunk, chunk)
        for j in range(hb):
            sl = slice(j * HEAD_DIM, (j + 1) * HEAD_DIM)
            state = state_s[j]
            sb = state.astype(BF16)
            ks = _dot(kq_s[j, c], sb)
            out = ks[HEAD_DIM:] + op_s[j, c]
            state_s[j] = state * eg_s[j, c][0:1, :] + nc_s[j, c] - ks[:HEAD_DIM]
            silu_z = z_ref[pl.ds(r0, chunk), sl].astype(F32)
            o_ref[pl.ds(r0, chunk), sl] = (_rms(out, onw_ref[...]) * silu_z).astype(BF16)
        return carry

    lax.fori_loop(0, n_chunks, step, 0, unroll=STEP_UNROLL)


def _deltanet(qkv, z, gb3, gab, conv_w, o_norm_w, batch, seq, heads, *, hb=4, group=8):
    m = qkv.shape[0]
    mix = heads * HEAD_DIM
    chunk = DN_CHUNK
    group = min(group, seq // chunk)
    n_chunks = seq // chunk
    hgroups = heads // hb
    wblk = hb * HEAD_DIM
    col = lambda off: pl.BlockSpec((seq, wblk), lambda b, h: (b, off + h))
    cw = lambda off: pl.BlockSpec((CONV_WIDTH, wblk), lambda b, h: (0, off + h))
    return pl.pallas_call(
        functools.partial(_dn_kernel, hb=hb, chunk=chunk, group=group, heads=heads),
        grid=(batch, hgroups),
        in_specs=[
            col(0), col(hgroups), col(2 * hgroups),
            pl.BlockSpec((seq, wblk), lambda b, h: (b, h)),
            pl.BlockSpec((seq, 3 * LANES), lambda b, h: (b, 0)),
            pl.BlockSpec((seq, 2 * LANES), lambda b, h: (b, 0)),
            cw(0), cw(hgroups), cw(2 * hgroups),
            pl.BlockSpec((1, HEAD_DIM), lambda b, h: (0, 0)),
        ],
        out_specs=pl.BlockSpec((seq, wblk), lambda b, h: (b, h)),
        out_shape=jax.ShapeDtypeStruct((m, mix), BF16),
        scratch_shapes=[
            pltpu.VMEM((hb, 3 * LANES, 2 * LANES), BF16),
            pltpu.VMEM((3, group * chunk + BF16_ROWS, HEAD_DIM), F32),
            pltpu.VMEM((hb, n_chunks, HEAD_DIM + chunk, HEAD_DIM), BF16),
            pltpu.VMEM((hb, n_chunks, HEAD_DIM, HEAD_DIM), F32),
            pltpu.VMEM((hb, n_chunks, chunk, HEAD_DIM), F32),
            pltpu.VMEM((hb, n_chunks, 8, LANES), F32),
            pltpu.VMEM((hb, HEAD_DIM, HEAD_DIM), F32),
        ],
        compiler_params=pltpu.CompilerParams(
            dimension_semantics=("arbitrary", "arbitrary"), vmem_limit_bytes=VMEM_LIMIT),
        name="deltanet",
    )(qkv, qkv, qkv, z, gb3, gab, conv_w, conv_w, conv_w, o_norm_w.reshape(1, HEAD_DIM))


def _fox_kernel(q_ref, k_ref, v_ref, gate_ref, fab_ref, o_ref, qa_s, ka_s, vt_s, *, tq, hb):
    seq = q_ref.shape[0]
    n_q = seq // tq
    lane = lax.broadcasted_iota(jnp.int32, (seq, LANES), 1)
    key_le_query = (lax.broadcasted_iota(jnp.int32, (tq, tq), 0)
                    <= lax.broadcasted_iota(jnp.int32, (tq, tq), 1))

    def col_max(m, s):
        sm = jnp.max(s, axis=0, keepdims=True)
        return sm if m is None else jnp.maximum(m, sm)

    def head_program(j):
        sl = slice(j * HEAD_DIM, (j + 1) * HEAD_DIM)
        head = pl.program_id(1) * hb + j
        fb = fab_ref[:, LANES:]
        qa_s[j, :, :HEAD_DIM] = q_ref[:, sl]
        qa_s[j, :, HEAD_DIM:] = fab_ref[:, :LANES]
        ka_s[j, :, :HEAD_DIM] = k_ref[:, sl]
        ka_s[j, :, HEAD_DIM:] = jnp.where(jnp.right_shift(lane, 3) == head, fb, jnp.zeros_like(fb))
        vt_s[j, :HEAD_DIM, :] = v_ref[:, sl].astype(F32).T.astype(BF16)
        vt_s[j, HEAD_DIM:, :] = jnp.where(
            lax.broadcasted_iota(jnp.int32, (BF16_ROWS, seq), 0) == 0, 1.0, 0.0).astype(BF16)

        def score_tile(qi, kb):
            s = _dot_nt(ka_s[j, kb * tq:(kb + 1) * tq, :], qa_s[j, qi * tq:(qi + 1) * tq, :])
            return jnp.where(key_le_query, s, -jnp.inf) if kb == qi else s

        tiles = [score_tile(0, 0)]
        m = col_max(None, tiles[0])
        yield
        for qi in range(n_q):
            lo, hi = qi * tq, (qi + 1) * tq
            nxt_tiles, nxt_m, p_tiles = [], None, []
            n_next = qi + 2 if qi + 1 < n_q else 0
            for t in range(max(n_next, qi + 1)):
                if t < n_next:
                    nxt_tiles.append(score_tile(qi + 1, t))
                    nxt_m = col_max(nxt_m, nxt_tiles[-1])
                if t <= qi:
                    p_tiles.append(jnp.exp2(tiles[t] - m).astype(BF16))
            pt = p_tiles[0] if qi == 0 else jnp.concatenate(p_tiles, axis=0)
            ot = _dot(vt_s[j, :, :hi], pt)
            gate = gate_ref[lo:hi, sl].astype(F32)
            o = ot[:HEAD_DIM] / ot[HEAD_DIM:HEAD_DIM + 1]
            o_ref[lo:hi, sl] = (o.T * gate).astype(BF16)
            tiles, m = nxt_tiles, nxt_m
            yield

    programs = [head_program(j) for j in range(hb)]
    for _ in range(n_q + 1):
        for prog in programs:
            next(prog)


def _fox_attention(q, k, v, gate, fab, batch, seq, heads, *, tq=256, hb=4):
    m, mix = q.shape
    tq = min(tq, seq)
    col = pl.BlockSpec((seq, hb * HEAD_DIM), lambda b, h: (b, h))
    return pl.pallas_call(
        functools.partial(_fox_kernel, tq=tq, hb=hb),
        grid=(batch, heads // hb),
        in_specs=[col, col, col, col, pl.BlockSpec((seq, 2 * LANES), lambda b, h: (b, 0))],
        out_specs=col,
        out_shape=jax.ShapeDtypeStruct((m, mix), BF16),
        scratch_shapes=[
            pltpu.VMEM((hb, seq, 2 * HEAD_DIM), BF16),
            pltpu.VMEM((hb, seq, 2 * HEAD_DIM), BF16),
            pltpu.VMEM((hb, HEAD_DIM + BF16_ROWS, seq), BF16),
        ],
        compiler_params=pltpu.CompilerParams(
            dimension_semantics=("arbitrary", "arbitrary"), vmem_limit_bytes=VMEM_LIMIT),
        name="fox_attention",
    )(q, k, v, gate, fab)


def _post_kernel(x_ref, mix_ref, mo_ref, wo_ref, nw_ref, w1_ref, w2_ref, y_ref, hid_s, *, f_chunk):
    mixw = mix_ref.shape[1]
    x1 = x_ref[...] + _dot(mix_ref[...], wo_ref[:mixw, :]) + _dot(mo_ref[...], wo_ref[mixw:, :])
    h = _rms(x1, nw_ref[...]).astype(BF16)
    for f0 in range(0, w1_ref.shape[1], f_chunk):
        t = jnp.maximum(_dot(h, w1_ref[:, f0:f0 + f_chunk]), 0.0)
        hid_s[:, f0:f0 + f_chunk] = (t * t).astype(BF16)
    y_ref[...] = x1 + _dot(hid_s[...], w2_ref[...])


def _post(x2, mix, mem_out, w_out, norm_w, w1, w2, layer, *, tm=1024, f_chunk=512):
    m, d = x2.shape
    mixw, mw, dff = mix.shape[1], mem_out.shape[1], w1.shape[2]
    row = lambda n: pl.BlockSpec((tm, n), lambda i: (i, 0))
    layer_spec = lambda r, c: pl.BlockSpec(
        (None, r, c), lambda i: (layer, 0, 0), pipeline_mode=pl.Buffered(1))
    return pl.pallas_call(
        functools.partial(_post_kernel, f_chunk=f_chunk),
        grid=(m // tm,),
        in_specs=[
            row(d), row(mixw), row(mw), layer_spec(mixw + mw, d), _const_spec((1, d)),
            layer_spec(d, dff), layer_spec(dff, d),
        ],
        out_specs=row(d),
        out_shape=jax.ShapeDtypeStruct((m, d), F32),
        scratch_shapes=[pltpu.VMEM((tm, dff), BF16)],
        compiler_params=pltpu.CompilerParams(
            dimension_semantics=("arbitrary",), vmem_limit_bytes=VMEM_LIMIT),
        name="post",
    )(x2, mix, mem_out, w_out, norm_w.reshape(1, d), w1, w2)


def kernel(x, mem, mem_norm_w, w_mem_kv, mem_k_norm_w, norm1_w, dn_w_in, dn_conv_w, dn_a_log,
           dn_dt_bias, dn_o_norm_w, fox_w_in, fox_f_bias, fox_q_norm_w, fox_k_norm_w, memq_norm_w,
           w_out, norm2_w, w_mlp1, w_mlp2):
    batch, seq, d = x.shape
    depth = norm1_w.shape[0]
    mem_k, mem_v = _mem_kv(mem, mem_norm_w, w_mem_kv, mem_k_norm_w)
    x2 = x.reshape(batch * seq, d)
    w_out_b, w_mlp1_b, w_mlp2_b = w_out.astype(BF16), w_mlp1.astype(BF16), w_mlp2.astype(BF16)
    for i in range(depth):
        j = i // 2
        if i % 2 == 0:
            heads = dn_a_log.shape[1]
            qkv, z, gb3, gab, mem_out = _in_proj_dn(
                x2, seq, norm1_w[i], dn_w_in[j], dn_a_log[j], dn_dt_bias[j], memq_norm_w[i],
                mem_k, mem_v)
            mix = _deltanet(qkv, z, gb3, gab, dn_conv_w[j], dn_o_norm_w[j], batch, seq, heads)
        else:
            heads = fox_f_bias.shape[1]
            q, k, v, gate, fab, mem_out = _in_proj_fox(
                x2, seq, norm1_w[i], fox_w_in[j], fox_f_bias[j], fox_q_norm_w[j],
                fox_k_norm_w[j], memq_norm_w[i], mem_k, mem_v)
            mix = _fox_attention(q, k, v, gate, fab, batch, seq, heads)
        x2 = _post(x2, mix, mem_out, w_out_b, norm2_w[i], w_mlp1_b, w_mlp2_b, i)
    return x2.reshape(batch, seq, d)
```
